```python
import jax, jax.numpy as jnp
from jax import lax
import numpy as np

D_MODEL = 1024
BATCH = 32
SEQ = 256
DEPTH = 4
DEC_BATCH = 4
DEC_SEQ = 2048
PAST_LEN = 512

GRID_W = 64
N_MIXERS = 2
N_A_LAYERS = (DEPTH + 1) // 2
N_B_LAYERS = DEPTH // 2
A_HEADS = 8
A_DK = D_MODEL // A_HEADS
A_DV = D_MODEL // A_HEADS
A_FDIM = A_HEADS * A_DK
A_VDIM = A_HEADS * A_DV
A_IN = 3 * A_FDIM + 2 * A_VDIM
CHUNK = 32
ATT_HEADS = 8
KV_HEADS = 2
GROUP = ATT_HEADS // KV_HEADS
HEAD_DIM = D_MODEL // ATT_HEADS
QKV_DIM = (ATT_HEADS + 2 * KV_HEADS) * HEAD_DIM
WINDOW = 128
BLOCK = 128
D_FF = ((8 * D_MODEL // 3 + 255) // 256) * 256
ROPE_BASE = 10000.0
EPS = 1e-6
F32 = jnp.float32

kernel_name = 'hybrid_hgrn2_swa_diffusion_step'


def rmsnorm(x, g):
    xf = x.astype(F32)
    y = xf * lax.rsqrt(jnp.mean(xf * xf, axis=-1, keepdims=True) + EPS)
    return (y * g.astype(F32)).astype(x.dtype)


def modulate(h, shift, scale):
    return h * (1 + scale) + shift


def adaln(cond, w, b):
    mod = jax.nn.silu(cond) @ w + b
    return jnp.split(mod[:, None, :], 6, axis=-1)


def swiglu(h, w_gu, w_d):
    g, u = jnp.split(h @ w_gu, 2, axis=-1)
    return (jax.nn.silu(g) * u) @ w_d


def rope_1d(x, pos):
    half = x.shape[-1] // 2
    inv = ROPE_BASE ** (-jnp.arange(half, dtype=F32) / half)
    ang = pos[:, None] * inv[None, :]
    cos = jnp.cos(ang)[None, :, None, :]
    sin = jnp.sin(ang)[None, :, None, :]
    xf = x.astype(F32)
    x1, x2 = xf[..., :half], xf[..., half:]
    return jnp.concatenate([x1 * cos - x2 * sin, x1 * sin + x2 * cos], axis=-1).astype(x.dtype)


def axial_rope(x, row_pos, col_pos):
    h = x.shape[-1] // 2
    return jnp.concatenate([rope_1d(x[..., :h], row_pos), rope_1d(x[..., h:], col_pos)], axis=-1)


def grid_positions(n):
    rows = n // GRID_W
    row_pos = jnp.repeat(jnp.arange(rows, dtype=F32), GRID_W)
    col_pos = jnp.tile(jnp.arange(GRID_W, dtype=F32), rows)
    return row_pos, col_pos


def hgrn_lower_bounds(lb_param):
    p = jax.nn.softmax(lb_param.astype(F32), axis=0)
    cs = jnp.cumsum(p, axis=0)
    return cs - cs[:1]


def gla_scan(q, k, v, logf, s0):
    B, N, H, _ = q.shape
    nc = N // CHUNK

    def to_chunks(a):
        return a.reshape(B, nc, CHUNK, H, a.shape[-1]).transpose(1, 0, 3, 2, 4)

    causal = jnp.tril(jnp.ones((CHUNK, CHUNK), dtype=bool))[:, :, None]

    def step(S, inp):
        qc, kc, vc, gc = inp
        b = jnp.cumsum(gc, axis=2)
        o_inter = jnp.einsum('bhtd,bhde->bhte', qc * jnp.exp(b), S)
        rel = b[:, :, :, None, :] - b[:, :, None, :, :]
        decay = jnp.exp(jnp.where(causal, rel, -jnp.inf))
        scores = jnp.einsum('bhtd,bhsd,bhtsd->bhts', qc, kc, decay)
        o_intra = jnp.einsum('bhts,bhse->bhte', scores, vc)
        b_last = b[:, :, -1]
        k_dec = kc * jnp.exp(b_last[:, :, None, :] - b)
        S_new = jnp.exp(b_last)[..., None] * S + jnp.einsum('bhsd,bhse->bhde', k_dec, vc)
        return S_new, o_inter + o_intra

    s_fin, o = lax.scan(step, s0, (to_chunks(q), to_chunks(k), to_chunks(v), to_chunks(logf)))
    o = o.transpose(1, 0, 3, 2, 4).reshape(B, N, H, v.shape[-1])
    return o, s_fin


def hgrn_mixer(h, w_in, gnorm, w_out, lb, s0):
    B, N, _ = h.shape
    proj = h @ w_in
    q, zf, zb, iv, g = jnp.split(proj, [A_FDIM, 2 * A_FDIM, 3 * A_FDIM, 3 * A_FDIM + A_VDIM], axis=-1)

    def heads(a, d):
        return a.reshape(B, N, A_HEADS, d)

    def gate(z, lbd):
        f = lbd + (1 - lbd) * jax.nn.sigmoid(z.astype(F32))
        return 1 - f, jnp.log(f)

    qf = jax.nn.silu(heads(q, A_DK).astype(F32))
    v = heads(iv, A_DV).astype(F32)
    k_fw, g_fw = gate(heads(zf, A_DK), lb[0].reshape(A_HEADS, A_DK))
    k_bw, g_bw = gate(heads(zb, A_DK), lb[1].reshape(A_HEADS, A_DK))
    s0f = s0.astype(F32)

    def rev(a):
        return jnp.flip(a, axis=1)

    o_fw, s_fw = gla_scan(qf, k_fw, v, g_fw, s0f[:, 0])
    o_bw, s_bw = gla_scan(rev(qf), rev(k_bw), rev(v), rev(g_bw), s0f[:, 1])
    o = o_fw + rev(o_bw)
    o = o * lax.rsqrt(jnp.mean(o * o, axis=-1, keepdims=True) + EPS) * gnorm.reshape(A_HEADS, A_DV).astype(F32)
    o = o.astype(h.dtype) * jax.nn.silu(heads(g, A_DV))
    out = o.reshape(B, N, A_VDIM) @ w_out
    return out, jnp.stack([s_fw, s_bw], axis=1).astype(s0.dtype)


def qkv_heads(h, w_qkv):
    B, N, _ = h.shape
    proj = h @ w_qkv
    q = proj[..., :ATT_HEADS * HEAD_DIM].reshape(B, N, ATT_HEADS, HEAD_DIM)
    k = proj[..., ATT_HEADS * HEAD_DIM:(ATT_HEADS + KV_HEADS) * HEAD_DIM].reshape(B, N, KV_HEADS, HEAD_DIM)
    v = proj[..., (ATT_HEADS + KV_HEADS) * HEAD_DIM:].reshape(B, N, KV_HEADS, HEAD_DIM)
    return q, k, v


def sink_logits(sink, shape):
    s = sink.reshape(KV_HEADS, GROUP).astype(F32)[:, :, None, None]
    return jnp.broadcast_to(s, shape[:-1] + (1,))


def attn_context(h, w_qkv, w_o, sink):
    B, S, _ = h.shape
    q, k, v = qkv_heads(h, w_qkv)
    nq = S // BLOCK
    qb = q.reshape(B, nq, BLOCK, KV_HEADS, GROUP, HEAD_DIM).transpose(1, 0, 2, 3, 4, 5)
    scale = HEAD_DIM ** -0.5

    def one_block(qi):
        lg = jnp.einsum('bqkgd,bskd->bkgqs', qi, k).astype(F32) * scale
        p = jax.nn.softmax(jnp.concatenate([sink_logits(sink, lg.shape), lg], axis=-1), axis=-1)
        return jnp.einsum('bkgqs,bskd->bqkgd', p[..., 1:].astype(h.dtype), v)

    o = lax.map(one_block, qb)
    o = o.transpose(1, 0, 2, 3, 4, 5).reshape(B, S, ATT_HEADS * HEAD_DIM)
    return o @ w_o, k, v


def attn_latent(h, w_qkv, w_o, sink, k_ctx, v_ctx, row_pos, col_pos):
    B, N, _ = h.shape
    q, k, v = qkv_heads(h, w_qkv)
    q = axial_rope(q, row_pos, col_pos)
    k = axial_rope(k, row_pos, col_pos)
    nb = N // BLOCK
    qb = q.reshape(B, nb, BLOCK, KV_HEADS, GROUP, HEAD_DIM)

    def band(a):
        ap = jnp.pad(a, ((0, 0), (BLOCK, BLOCK), (0, 0), (0, 0))).reshape(B, nb + 2, BLOCK, KV_HEADS, HEAD_DIM)
        return jnp.concatenate([ap[:, :-2], ap[:, 1:-1], ap[:, 2:]], axis=2)

    kb, vb = band(k), band(v)
    qi = jnp.arange(BLOCK)[:, None]
    kj = jnp.arange(3 * BLOCK)[None, :]
    rel = kj - BLOCK - qi
    key_pos = (jnp.arange(nb)[:, None, None] - 1) * BLOCK + kj[None]
    mask = (jnp.abs(rel)[None] <= WINDOW) & (key_pos >= 0) & (key_pos < N)
    scale = HEAD_DIM ** -0.5
    lw = jnp.einsum('bnqkgd,bnskd->bnkgqs', qb, kb).astype(F32) * scale
    lw = jnp.where(mask[None, :, None, None], lw, -jnp.inf)
    lc = jnp.einsum('bnqkgd,bpkd->bnkgqp', qb, k_ctx).astype(F32) * scale
    p = jax.nn.softmax(jnp.concatenate([sink_logits(sink, lw.shape), lc, lw], axis=-1), axis=-1)
    P = k_ctx.shape[1]
    pc = p[..., 1:1 + P].astype(h.dtype)
    pw = p[..., 1 + P:].astype(h.dtype)
    o = jnp.einsum('bnkgqp,bpkd->bnqkgd', pc, v_ctx) + jnp.einsum('bnkgqs,bnskd->bnqkgd', pw, vb)
    return o.reshape(B, N, ATT_HEADS * HEAD_DIM) @ w_o


def setup_inputs(seed: int = 0) -> dict:
    key = jax.random.key(seed)
    ks = jax.random.split(key, 21)

    def nrm(k, shape, scale):
        return jax.random.normal(k, shape, F32) * scale

    return {
        'x_prompt': nrm(ks[0], (BATCH, SEQ, D_MODEL), 1.0),
        'x_sample': nrm(ks[1], (DEC_BATCH, DEC_SEQ, D_MODEL), 1.0),
        'cache_k': nrm(ks[2], (DEC_BATCH, N_B_LAYERS, PAST_LEN, KV_HEADS, HEAD_DIM), 1.0),
        'cache_v': nrm(ks[3], (DEC_BATCH, N_B_LAYERS, PAST_LEN, KV_HEADS, HEAD_DIM), 1.0),
        'state_hgrn': nrm(ks[4], (DEC_BATCH, N_A_LAYERS, 2, A_HEADS, A_DK, A_DV), 0.5),
        'c': nrm(ks[5], (DEC_BATCH, D_MODEL), 1.0),
        'c_ctx': nrm(ks[6], (D_MODEL,), 1.0),
        'w_ada': nrm(ks[7], (DEPTH, D_MODEL, 6 * D_MODEL), 0.5 * D_MODEL ** -0.5),
        'b_ada': nrm(ks[8], (DEPTH, 6 * D_MODEL), 0.01),
        'norm1': 1.0 + nrm(ks[9], (DEPTH, D_MODEL), 0.01),
        'norm2': 1.0 + nrm(ks[10], (DEPTH, D_MODEL), 0.01),
        'norm_final': 1.0 + nrm(ks[11], (D_MODEL,), 0.01),
        'w_gate_up': nrm(ks[12], (DEPTH, D_MODEL, 2 * D_FF), D_MODEL ** -0.5),
        'w_down': nrm(ks[13], (DEPTH, D_FF, D_MODEL), D_FF ** -0.5),
        'w_in_a': nrm(ks[14], (N_A_LAYERS, D_MODEL, A_IN), D_MODEL ** -0.5),
        'lower_bounds': nrm(ks[15], (N_A_LAYERS, 2, A_FDIM), 1.0),
        'gnorm_a': 1.0 + nrm(ks[16], (N_A_LAYERS, A_VDIM), 0.01),
        'w_out_a': nrm(ks[17], (N_A_LAYERS, A_VDIM, D_MODEL), A_VDIM ** -0.5),
        'w_qkv_b': nrm(ks[18], (N_B_LAYERS, D_MODEL, QKV_DIM), D_MODEL ** -0.5),
        'w_out_b': nrm(ks[19], (N_B_LAYERS, ATT_HEADS * HEAD_DIM, D_MODEL), (ATT_HEADS * HEAD_DIM) ** -0.5),
        'sink_b': nrm(ks[20], (N_B_LAYERS, ATT_HEADS), 1.0),
    }


def reference(x_prompt, x_sample, cache_k, cache_v, state_hgrn, c, c_ctx, w_ada, b_ada, norm1, norm2, norm_final,
              w_gate_up, w_down, w_in_a, lower_bounds, gnorm_a, w_out_a, w_qkv_b, w_out_b, sink_b):
    lb_all = hgrn_lower_bounds(lower_bounds)

    x = x_prompt
    cond_ctx = c_ctx[None, :]
    new_k, new_v, new_s = [], [], []
    for l in range(DEPTH):
        sh1, sc1, g1, sh2, sc2, g2 = adaln(cond_ctx, w_ada[l], b_ada[l])
        h = modulate(rmsnorm(x, norm1[l]), sh1, sc1)
        j = l // N_MIXERS
        if l % N_MIXERS == 0:
            s0 = jnp.zeros((x.shape[0], 2, A_HEADS, A_DK, A_DV), x.dtype)
            out, s_fin = hgrn_mixer(h, w_in_a[j], gnorm_a[j], w_out_a[j], lb_all[j], s0)
            new_s.append(s_fin)
        else:
            out, k_c, v_c = attn_context(h, w_qkv_b[j], w_out_b[j], sink_b[j])
            new_k.append(k_c)
            new_v.append(v_c)
        x = x + g1 * out
        x = x + g2 * swiglu(modulate(rmsnorm(x, norm2[l]), sh2, sc2), w_gate_up[l], w_down[l])
    y_prompt = rmsnorm(x, norm_final)
    new_cache_k = jnp.stack(new_k, axis=1)
    new_cache_v = jnp.stack(new_v, axis=1)
    new_state_hgrn = jnp.stack(new_s, axis=1)

    x = x_sample
    row_pos, col_pos = grid_positions(x_sample.shape[1])
    for l in range(DEPTH):
        sh1, sc1, g1, sh2, sc2, g2 = adaln(c, w_ada[l], b_ada[l])
        h = modulate(rmsnorm(x, norm1[l]), sh1, sc1)
        j = l // N_MIXERS
        if l % N_MIXERS == 0:
            out, _ = hgrn_mixer(h, w_in_a[j], gnorm_a[j], w_out_a[j], lb_all[j], state_hgrn[:, j])
        else:
            out = attn_latent(h, w_qkv_b[j], w_out_b[j], sink_b[j], cache_k[:, j], cache_v[:, j], row_pos, col_pos)
        x = x + g1 * out
        x = x + g2 * swiglu(modulate(rmsnorm(x, norm2[l]), sh2, sc2), w_gate_up[l], w_down[l])
    y_sample = rmsnorm(x, norm_final)

    return (y_prompt, y_sample, new_cache_k, new_cache_v, new_state_hgrn)
```

```python
import functools

import jax
import jax.numpy as jnp
from jax import lax
from jax.experimental import pallas as pl
from jax.experimental.pallas import tpu as pltpu

F32 = jnp.float32
BF16 = jnp.bfloat16
EPS = 1e-6
ROPE_BASE = 10000.0

V7X_VMEM_LIMIT_BYTES = 56 * 1024 * 1024
SUBLANES = 8
LANES = 128

A_HEADS = 8
ATT_HEADS = 8
KV_HEADS = 2
GROUP = ATT_HEADS // KV_HEADS
WINDOW = 128
BLOCK = 128
GRID_W = 64
GLA_CHUNK = 128
GLA_LEAF = SUBLANES
NEG_BIG = -1e30


def _params(*sem):
    return pltpu.CompilerParams(dimension_semantics=sem, vmem_limit_bytes=V7X_VMEM_LIMIT_BYTES)


def _resident(shape):
    nd = len(shape)
    return pl.BlockSpec(shape, lambda *_: (0,) * nd, pipeline_mode=pl.Buffered(1))


def _div(x, k):
    assert k & (k - 1) == 0
    return jnp.right_shift(x, k.bit_length() - 1)


def _mod(x, k):
    assert k & (k - 1) == 0
    return jnp.bitwise_and(x, k - 1)


def _sigmoid(x):
    return 1.0 / (1.0 + jnp.exp(-x))


def _silu(x):
    return x * _sigmoid(x)


def _rmsnorm(x, gain):
    return x * lax.rsqrt(jnp.mean(x * x, axis=-1, keepdims=True) + EPS) * gain


def _dot(a, b):
    return jnp.dot(a, b, preferred_element_type=F32)


def _dot_nt(a, b):
    return lax.dot_general(a, b, (((1,), (1,)), ((), ())), preferred_element_type=F32)


def _ada_kernel(cond_ref, w_ref, b_ref, out_ref):
    s = _silu(cond_ref[...]).astype(BF16)
    out_ref[...] = _dot(s, w_ref[...].astype(BF16)) + b_ref[...]


def _ada_call(cond8, w_ada, b_ada):
    depth, d, n = w_ada.shape
    tn = n // 4
    return pl.pallas_call(
        _ada_kernel,
        grid=(depth, n // tn),
        in_specs=[
            pl.BlockSpec((SUBLANES, d), lambda l, j: (0, 0)),
            pl.BlockSpec((None, d, tn), lambda l, j: (l, 0, j)),
            pl.BlockSpec((None, 1, tn), lambda l, j: (l, 0, j)),
        ],
        out_specs=pl.BlockSpec((None, SUBLANES, tn), lambda l, j: (l, 0, j)),
        out_shape=jax.ShapeDtypeStruct((depth, SUBLANES, n), F32),
        compiler_params=_params("parallel", "parallel"),
        name="adaln_mod",
    )(cond8, w_ada, b_ada.reshape(depth, 1, n))


def _proj_kernel(x_ref, mod_ref, gain_ref, w_ref, out_ref, *, d):
    shift = mod_ref[:, 0:d]
    scale = mod_ref[:, d:2 * d]
    h = _rmsnorm(x_ref[...], gain_ref[...]) * (1.0 + scale) + shift
    out_ref[...] = _dot(h.astype(BF16), w_ref[...])


def _proj_call(x, mod, gain, w, cond_row, tm, name):
    t, d = x.shape
    n = w.shape[1]
    return pl.pallas_call(
        functools.partial(_proj_kernel, d=d),
        grid=(t // tm,),
        in_specs=[
            pl.BlockSpec((tm, d), lambda i: (i, 0)),
            pl.BlockSpec((None, 1, mod.shape[-1]), lambda i: (cond_row(i * tm), 0, 0)),
            _resident((1, d)),
            _resident((d, n)),
        ],
        out_specs=pl.BlockSpec((tm, n), lambda i: (i, 0)),
        out_shape=jax.ShapeDtypeStruct((t, n), F32),
        compiler_params=_params("parallel"),
        name=name,
    )(x, mod, gain, w)


def _ffn_kernel(x_ref, a_ref, mod_ref, gain_ref, wo_ref, wgu_ref, wd_ref, nf_ref, *out_refs,
                d, d_ff, ff_chunk, final):
    g1 = mod_ref[:, 2 * d:3 * d]
    sh2 = mod_ref[:, 3 * d:4 * d]
    sc2 = mod_ref[:, 4 * d:5 * d]
    g2 = mod_ref[:, 5 * d:6 * d]
    x1 = x_ref[...] + g1 * _dot(a_ref[...].astype(BF16), wo_ref[...])
    h = (_rmsnorm(x1, gain_ref[...]) * (1.0 + sc2) + sh2).astype(BF16)
    y = None
    for j in range(d_ff // ff_chunk):
        lo = j * ff_chunk
        g = _dot(h, wgu_ref[:, lo:lo + ff_chunk])
        u = _dot(h, wgu_ref[:, d_ff + lo:d_ff + lo + ff_chunk])
        part = _dot((_silu(g) * u).astype(BF16), wd_ref[lo:lo + ff_chunk, :])
        y = part if y is None else y + part
    x2 = x1 + g2 * y
    out_refs[0][...] = x2
    if final:
        out_refs[1][...] = _rmsnorm(x2, nf_ref[...])


def _ffn_call(x, a, mod, gain, w_out, w_gu, w_d, norm_final, cond_row, tm, final):
    t, d = x.shape
    d_ff = w_d.shape[0]
    ff_chunk = d_ff // 2
    row = pl.BlockSpec((tm, d), lambda i: (i, 0))
    out_shape = [jax.ShapeDtypeStruct((t, d), F32)]
    out_specs = [row]
    if final:
        out_shape.append(jax.ShapeDtypeStruct((t, d), F32))
        out_specs.append(row)
    outs = pl.pallas_call(
        functools.partial(_ffn_kernel, d=d, d_ff=d_ff, ff_chunk=ff_chunk, final=final),
        grid=(t // tm,),
        in_specs=[
            row,
            row,
            pl.BlockSpec((None, 1, mod.shape[-1]), lambda i: (cond_row(i * tm), 0, 0)),
            _resident((1, d)),
            _resident(w_out.shape),
            _resident(w_gu.shape),
            _resident(w_d.shape),
            _resident((1, d)),
        ],
        out_specs=out_specs,
        out_shape=out_shape,
        compiler_params=_params("parallel"),
        name="outproj_ffn",
    )(x, a, mod, gain, w_out, w_gu, w_d, norm_final)
    return outs


def _row_blocks(b, rows, size):
    parts = [jnp.broadcast_to(b[r:r + 1, :], (size, b.shape[1])) for r in rows]
    return parts[0] if len(parts) == 1 else jnp.concatenate(parts, axis=0)


def _gla_chunk(q, k, lg, v, st, reverse):
    c = q.shape[0]
    dk = q.shape[1]
    t_idx = lax.broadcasted_iota(jnp.int32, (c, c), 0)
    s_idx = lax.broadcasted_iota(jnp.int32, (c, c), 1)
    tri = (s_idx >= t_idx) if reverse else (s_idx <= t_idx)
    tri_b = jnp.where(tri, 1.0, 0.0).astype(BF16)
    hi = lg.astype(BF16)
    r1 = lg - hi.astype(F32)
    mid = r1.astype(BF16)
    lo = (r1 - mid.astype(F32)).astype(BF16)
    b = _dot(tri_b, hi) + _dot(tri_b, mid) + _dot(tri_b, lo)

    b_end = b[0:1, :] if reverse else b[c - 1:c, :]
    st_b = st.astype(BF16)
    o = _dot_nt((q * jnp.exp(b)).astype(BF16), st_b)
    k_dec = (k * jnp.exp(b_end - b)).astype(BF16)
    v_b = v.astype(BF16)
    st_new = st * jnp.exp(b_end) + _dot(v.T.astype(BF16), k_dec)

    pos = lax.broadcasted_iota(jnp.int32, (c, dk), 0)
    scores = jnp.zeros((c, c), F32)
    m = c // 2
    while m >= GLA_LEAF:
        nblk = c // (2 * m)
        bound = [kb * 2 * m + (m if reverse else m - 1) for kb in range(nblk)]
        e = jnp.exp(-jnp.abs(b - _row_blocks(b, bound, 2 * m)))
        late = _mod(pos, 2 * m) >= m
        q_side = jnp.logical_not(late) if reverse else late
        ql = jnp.where(q_side, q * e, 0.0).astype(BF16)
        kl = jnp.where(q_side, 0.0, k * e).astype(BF16)
        s_l = _dot_nt(ql, kl)
        if nblk > 1:
            s_l = jnp.where(_div(t_idx, 2 * m) == _div(s_idx, 2 * m), s_l, 0.0)
        scores = scores + s_l
        m //= 2
    nleaf = c // GLA_LEAF
    k3 = k.reshape(nleaf, GLA_LEAF, dk)
    b3 = b.reshape(nleaf, GLA_LEAF, dk)
    sel_d = _mod(lax.broadcasted_iota(jnp.int32, (dk, c), 1), GLA_LEAF)
    r = jnp.zeros((c, c), F32)
    for sl in range(GLA_LEAF):
        ks = jnp.broadcast_to(k3[:, sl:sl + 1, :], k3.shape).reshape(c, dk)
        bs = jnp.broadcast_to(b3[:, sl:sl + 1, :], b3.shape).reshape(c, dk)
        term = (q * ks * jnp.exp(jnp.minimum(b - bs, 0.0))).astype(BF16)
        r = r + _dot(term, jnp.where(sel_d == sl, 1.0, 0.0).astype(BF16))
    same_leaf = _div(t_idx, GLA_LEAF) == _div(s_idx, GLA_LEAF)
    scores = scores + jnp.where(jnp.logical_and(same_leaf, tri), r, 0.0)
    o = o + _dot(scores.astype(BF16), v_b)
    return o, st_new


def _gla_kernel(*refs, n, has_s0, want_state):
    q_ref, zf_ref, zb_ref, v_ref, g_ref, lb_ref, gn_ref = refs[:7]
    pos = 7
    s0_ref = None
    if has_s0:
        s0_ref = refs[pos]
        pos += 1
    o_ref = refs[pos]
    pos += 1
    sfin_ref = None
    if want_state:
        sfin_ref = refs[pos]
        pos += 1
    oacc_ref, st_ref = refs[pos:pos + 2]
    c = GLA_CHUNK
    nchunks = n // c
    gn = gn_ref[...]

    for direction in (0, 1):
        z_ref = zf_ref if direction == 0 else zb_ref
        lb = lb_ref[direction:direction + 1, :]
        if has_s0:
            st_ref[...] = s0_ref[direction].T
        else:
            st_ref[...] = jnp.zeros(st_ref.shape, F32)

        def body(i, carry, direction=direction, z_ref=z_ref, lb=lb):
            ci = i if direction == 0 else nchunks - 1 - i
            rows = pl.ds(pl.multiple_of(ci * c, c), c)
            q = _silu(q_ref[rows, :])
            f = lb + (1.0 - lb) * _sigmoid(z_ref[rows, :])
            o, st_new = _gla_chunk(q, 1.0 - f, jnp.log(f), v_ref[rows, :], st_ref[...], direction == 1)
            st_ref[...] = st_new
            if direction == 0:
                oacc_ref[rows, :] = o
            else:
                o = oacc_ref[rows, :] + o
                o = o * lax.rsqrt(jnp.mean(o * o, axis=-1, keepdims=True) + EPS) * gn
                o_ref[rows, :] = o * _silu(g_ref[rows, :])
            return carry

        lax.fori_loop(0, nchunks, body, 0)
        if want_state:
            sfin_ref[direction] = st_ref[...].T


def _gla_call(proj, lb, gnorm, s0, row0, nseq, n, want_state, name):
    hd = LANES
    heads = A_HEADS
    blk0 = row0 // n

    def col(kind):
        return pl.BlockSpec((n, hd), lambda b, h: (blk0 + b, kind * heads + h))

    in_specs = [col(0), col(1), col(2), col(3), col(4),
                pl.BlockSpec((2, hd), lambda b, h: (0, h)),
                pl.BlockSpec((1, hd), lambda b, h: (0, h))]
    args = [proj, proj, proj, proj, proj, lb, gnorm]
    state_spec = pl.BlockSpec((None, 2, None, hd, hd), lambda b, h: (b, 0, h, 0, 0))
    if s0 is not None:
        in_specs.append(state_spec)
        args.append(s0)
    out_shape = [jax.ShapeDtypeStruct((nseq * n, heads * hd), F32)]
    out_specs = [pl.BlockSpec((n, hd), lambda b, h: (b, h))]
    if want_state:
        out_shape.append(jax.ShapeDtypeStruct((nseq, 2, heads, hd, hd), F32))
        out_specs.append(state_spec)
    return pl.pallas_call(
        functools.partial(_gla_kernel, n=n, has_s0=s0 is not None, want_state=want_state),
        grid=(nseq, heads),
        in_specs=in_specs,
        out_specs=out_specs,
        out_shape=out_shape,
        scratch_shapes=[pltpu.VMEM((n, hd), F32), pltpu.VMEM((hd, hd), F32)],
        compiler_params=_params("parallel", "parallel"),
        name=name,
    )(*args)


def _stack_groups(ref):
    return jnp.concatenate([ref[:, g * LANES:(g + 1) * LANES] for g in range(GROUP)], axis=0)


def _sink_column(sink_ref, kh, rows):
    gi = _div(lax.broadcasted_iota(jnp.int32, (GROUP * rows, 1), 0), rows)
    col = jnp.zeros((GROUP * rows, 1), F32)
    for g in range(GROUP):
        col = jnp.where(gi == g, sink_ref[kh, g], col)
    return col


def _attn_ctx_kernel(sink_ref, q_ref, k_ref, v_ref, o_ref, *, scale):
    kh = pl.program_id(1)
    rows = q_ref.shape[0]
    q = _stack_groups(q_ref).astype(BF16)
    lg = _dot_nt(q, k_ref[...].astype(BF16)) * scale
    sink = _sink_column(sink_ref, kh, rows)
    mx = jnp.maximum(jnp.max(lg, axis=-1, keepdims=True), sink)
    p = jnp.exp(lg - mx)
    den = jnp.sum(p, axis=-1, keepdims=True) + jnp.exp(sink - mx)
    o = _dot((p / den).astype(BF16), v_ref[...].astype(BF16))
    for g in range(GROUP):
        o_ref[:, g * LANES:(g + 1) * LANES] = o[g * rows:(g + 1) * rows, :]


def _attn_ctx_call(qkv, sink, nseq, n):
    hd = LANES
    kcol = ATT_HEADS
    vcol = ATT_HEADS + KV_HEADS
    return pl.pallas_call(
        functools.partial(_attn_ctx_kernel, scale=hd ** -0.5),
        grid=(nseq, KV_HEADS),
        in_specs=[
            pl.BlockSpec(memory_space=pltpu.SMEM),
            pl.BlockSpec((n, GROUP * hd), lambda b, kh: (b, kh)),
            pl.BlockSpec((n, hd), lambda b, kh: (b, kcol + kh)),
            pl.BlockSpec((n, hd), lambda b, kh: (b, vcol + kh)),
        ],
        out_specs=pl.BlockSpec((n, GROUP * hd), lambda b, kh: (b, kh)),
        out_shape=jax.ShapeDtypeStruct((nseq * n, ATT_HEADS * hd), F32),
        compiler_params=_params("parallel", "parallel"),
        name="attn_context",
    )(sink, qkv, qkv, qkv)


def _rope_kernel(x_ref, cos_ref, sa_ref, sb_ref, o_ref):
    cos = cos_ref[...]
    sa = sa_ref[...]
    sb = sb_ref[...]
    for h in range(x_ref.shape[1] // LANES):
        x = x_ref[:, h * LANES:(h + 1) * LANES]
        o_ref[:, h * LANES:(h + 1) * LANES] = (
            x * cos + pltpu.roll(x, LANES - 32, 1) * sa + pltpu.roll(x, 32, 1) * sb)


def _rope_call(qkv, tables, row0, nrows, n, tm):
    width = (ATT_HEADS + KV_HEADS) * LANES
    tab = pl.BlockSpec((tm, LANES), lambda i: (i % (n // tm), 0))
    return pl.pallas_call(
        _rope_kernel,
        grid=(nrows // tm,),
        in_specs=[pl.BlockSpec((tm, width), lambda i: (row0 // tm + i, 0)), tab, tab, tab],
        out_specs=pl.BlockSpec((tm, width), lambda i: (i, 0)),
        out_shape=jax.ShapeDtypeStruct((nrows, width), F32),
        compiler_params=_params("parallel"),
        name="axial_rope",
    )(qkv, *tables)


def _attn_lat_kernel(sink_ref, q_ref, k0_ref, k1_ref, k2_ref, v0_ref, v1_ref, v2_ref, kc_ref, vc_ref, o_ref,
                     *, scale, n):
    nb = pl.program_id(1)
    kh = pl.program_id(2)
    rows = q_ref.shape[0]
    q = _stack_groups(q_ref).astype(BF16)
    kw = jnp.concatenate([k0_ref[...], k1_ref[...], k2_ref[...]], axis=0).astype(BF16)
    vw = jnp.concatenate([v0_ref[...], v1_ref[...], v2_ref[...]], axis=0).astype(BF16)
    lc = _dot_nt(q, kc_ref[...].astype(BF16)) * scale
    lw = _dot_nt(q, kw) * scale
    qi = _mod(lax.broadcasted_iota(jnp.int32, lw.shape, 0), rows)
    kj = lax.broadcasted_iota(jnp.int32, lw.shape, 1)
    rel = kj - BLOCK - qi
    key_pos = (nb - 1) * BLOCK + kj
    valid = (jnp.abs(rel) <= WINDOW) & (key_pos >= 0) & (key_pos < n)
    lw = jnp.where(valid, lw, NEG_BIG)
    sink = _sink_column(sink_ref, kh, rows)
    mx = jnp.maximum(jnp.maximum(jnp.max(lc, axis=-1, keepdims=True), jnp.max(lw, axis=-1, keepdims=True)), sink)
    pc = jnp.exp(lc - mx)
    pw = jnp.where(valid, jnp.exp(lw - mx), 0.0)
    den = jnp.sum(pc, axis=-1, keepdims=True) + jnp.sum(pw, axis=-1, keepdims=True) + jnp.exp(sink - mx)
    o = _dot((pc / den).astype(BF16), vc_ref[...].astype(BF16)) + _dot((pw / den).astype(BF16), vw)
    for g in range(GROUP):
        o_ref[:, g * LANES:(g + 1) * LANES] = o[g * rows:(g + 1) * rows, :]


def _attn_lat_call(qk_rope, qkv, cache_k, cache_v, sink, row0, nseq, n):
    hd = LANES
    nblk = n // BLOCK
    past = cache_k.shape[1]
    vcol = ATT_HEADS + KV_HEADS
    vrow0 = row0 // BLOCK

    def near(off, base, colbase):
        def index(b, i, kh):
            return (base + b * nblk + jnp.clip(i + off, 0, nblk - 1), colbase + kh)
        return pl.BlockSpec((BLOCK, hd), index)

    cache_spec = pl.BlockSpec((None, past, hd), lambda b, i, kh: (b, 0, kh))
    return pl.pallas_call(
        functools.partial(_attn_lat_kernel, scale=hd ** -0.5, n=n),
        grid=(nseq, nblk, KV_HEADS),
        in_specs=[
            pl.BlockSpec(memory_space=pltpu.SMEM),
            pl.BlockSpec((BLOCK, GROUP * hd), lambda b, i, kh: (b * nblk + i, kh)),
            near(-1, 0, ATT_HEADS), near(0, 0, ATT_HEADS), near(1, 0, ATT_HEADS),
            near(-1, vrow0, vcol), near(0, vrow0, vcol), near(1, vrow0, vcol),
            cache_spec, cache_spec,
        ],
        out_specs=pl.BlockSpec((BLOCK, GROUP * hd), lambda b, i, kh: (b * nblk + i, kh)),
        out_shape=jax.ShapeDtypeStruct((nseq * n, ATT_HEADS * hd), F32),
        compiler_params=_params("parallel", "parallel", "parallel"),
        name="attn_latent",
    )(sink, qk_rope, qk_rope, qk_rope, qk_rope, qkv, qkv, qkv, cache_k, cache_v)


def _rope_tables(n):
    quarter = LANES // 4
    inv = ROPE_BASE ** (-jnp.arange(quarter, dtype=F32) / quarter)
    pos = jnp.arange(n)
    row_pos = (pos // GRID_W).astype(F32)
    col_pos = (pos % GRID_W).astype(F32)
    ang_r = row_pos[:, None] * inv[None, :]
    ang_c = col_pos[:, None] * inv[None, :]
    zero = jnp.zeros_like(ang_r)
    cos = jnp.concatenate([jnp.cos(ang_r), jnp.cos(ang_r), jnp.cos(ang_c), jnp.cos(ang_c)], axis=1)
    sin_up = jnp.concatenate([-jnp.sin(ang_r), zero, -jnp.sin(ang_c), zero], axis=1)
    sin_dn = jnp.concatenate([zero, jnp.sin(ang_r), zero, jnp.sin(ang_c)], axis=1)
    return cos, sin_up, sin_dn


def _lower_bounds(lb_param):
    p = jax.nn.softmax(lb_param.astype(F32), axis=0)
    cs = jnp.cumsum(p, axis=0)
    return cs - cs[:1]


def kernel(x_prompt, x_sample, cache_k, cache_v, state_hgrn, c, c_ctx, w_ada, b_ada, norm1, norm2, norm_final,
           w_gate_up, w_down, w_in_a, lower_bounds, gnorm_a, w_out_a, w_qkv_b, w_out_b, sink_b):
    batch, seq, d = x_prompt.shape
    dec_batch, dec_seq, _ = x_sample.shape
    depth = w_ada.shape[0]
    t_ctx = batch * seq
    t_lat = dec_batch * dec_seq
    assert dec_batch + 1 <= SUBLANES and dec_seq % seq == 0

    def cond_row(tok0):
        return jnp.where(tok0 < t_ctx, 0, 1 + (tok0 - t_ctx) // dec_seq)

    x = jnp.concatenate([x_prompt.reshape(t_ctx, d), x_sample.reshape(t_lat, d)], axis=0)
    cond8 = jnp.concatenate([c_ctx[None, :], c, jnp.zeros((SUBLANES - 1 - dec_batch, d), F32)], axis=0)
    mod = _ada_call(cond8, w_ada, b_ada).reshape(depth, SUBLANES, 1, 6 * d)
    lb_all = _lower_bounds(lower_bounds)
    rope_tabs = _rope_tables(dec_seq)
    past = cache_k.shape[2]
    nf = norm_final.reshape(1, d)

    new_k, new_v, new_s = [], [], []
    y = None
    for l in range(depth):
        j = l // 2
        n1 = norm1[l].reshape(1, d)
        n2 = norm2[l].reshape(1, d)
        if l % 2 == 0:
            proj = _proj_call(x, mod[l], n1, w_in_a[j].astype(BF16), cond_row, seq, "hgrn_in_proj")
            gn = gnorm_a[j].reshape(1, d)
            o_ctx, s_fin = _gla_call(proj, lb_all[j], gn, None, 0, batch, seq, True, "hgrn_scan_ctx")
            (o_lat,) = _gla_call(proj, lb_all[j], gn, state_hgrn[:, j], t_ctx, dec_batch, dec_seq, False,
                                 "hgrn_scan_lat")
            new_s.append(s_fin)
            a = jnp.concatenate([o_ctx, o_lat], axis=0)
            w_mix = w_out_a[j]
        else:
            qkv = _proj_call(x, mod[l], n1, w_qkv_b[j].astype(BF16), cond_row, seq, "attn_qkv_proj")
            sink = sink_b[j].reshape(KV_HEADS, GROUP)
            kv = qkv[:t_ctx, ATT_HEADS * LANES:].reshape(batch, seq, 2, KV_HEADS, LANES)
            new_k.append(kv[:, :, 0])
            new_v.append(kv[:, :, 1])
            a_ctx = _attn_ctx_call(qkv, sink, batch, seq)
            qk_rope = _rope_call(qkv, rope_tabs, t_ctx, t_lat, dec_seq, 512)
            a_lat = _attn_lat_call(qk_rope, qkv, cache_k[:, j].reshape(dec_batch, past, KV_HEADS * LANES),
                                   cache_v[:, j].reshape(dec_batch, past, KV_HEADS * LANES), sink,
                                   t_ctx, dec_batch, dec_seq)
            a = jnp.concatenate([a_ctx, a_lat], axis=0)
            w_mix = w_out_b[j]
        final = l == depth - 1
        outs = _ffn_call(x, a, mod[l], n2, w_mix.astype(BF16), w_gate_up[l].astype(BF16),
                         w_down[l].astype(BF16), nf, cond_row, 512, final)
        x = outs[0]
        if final:
            y = outs[1]

    y_prompt = y[:t_ctx].reshape(batch, seq, d)
    y_sample = y[t_ctx:].reshape(dec_batch, dec_seq, d)
    return (y_prompt, y_sample, jnp.stack(new_k, axis=1), jnp.stack(new_v, axis=1), jnp.stack(new_s, axis=1))
```

```python
import functools

import jax
import jax.numpy as jnp
from jax import lax
from jax.experimental import pallas as pl
from jax.experimental.pallas import tpu as pltpu

F32 = jnp.float32
BF16 = jnp.bfloat16
EPS = 1e-6
ROPE_BASE = 10000.0

V7X_VMEM_LIMIT_BYTES = 56 * 1024 * 1024
SUBLANES = 8
LANES = 128

A_HEADS = 8
ATT_HEADS = 8
KV_HEADS = 2
GROUP = ATT_HEADS // KV_HEADS
WINDOW = 128
BLOCK = 128
GRID_W = 64
GLA_CHUNK = 128
GLA_LEAF = SUBLANES
GLA_HEADS_PER_STEP = 2
GLA_BLOCK = 32
GLA_RANGE_BOUND = 60.0
NEG_BIG = -1e30


def _params(*sem):
    return pltpu.CompilerParams(dimension_semantics=sem, vmem_limit_bytes=V7X_VMEM_LIMIT_BYTES)


def _resident(shape):
    nd = len(shape)
    return pl.BlockSpec(shape, lambda *_: (0,) * nd, pipeline_mode=pl.Buffered(1))


def _div(x, k):
    assert k & (k - 1) == 0
    return jnp.right_shift(x, k.bit_length() - 1)


def _mod(x, k):
    assert k & (k - 1) == 0
    return jnp.bitwise_and(x, k - 1)


def _sigmoid(x):
    return 1.0 / (1.0 + jnp.exp(-x))


def _silu(x):
    return x * _sigmoid(x)


def _rmsnorm(x, gain):
    return x * lax.rsqrt(jnp.mean(x * x, axis=-1, keepdims=True) + EPS) * gain


def _dot(a, b):
    return jnp.dot(a, b, preferred_element_type=F32)


def _dot_nt(a, b):
    return lax.dot_general(a, b, (((1,), (1,)), ((), ())), preferred_element_type=F32)


def _ada_kernel(cond_ref, w_ref, b_ref, out_ref):
    s = _silu(cond_ref[...]).astype(BF16)
    out_ref[...] = _dot(s, w_ref[...].astype(BF16)) + b_ref[...]


def _ada_call(cond8, w_ada, b_ada):
    depth, d, n = w_ada.shape
    tn = n // 4
    return pl.pallas_call(
        _ada_kernel,
        grid=(depth, n // tn),
        in_specs=[
            pl.BlockSpec((SUBLANES, d), lambda l, j: (0, 0)),
            pl.BlockSpec((None, d, tn), lambda l, j: (l, 0, j)),
            pl.BlockSpec((None, 1, tn), lambda l, j: (l, 0, j)),
        ],
        out_specs=pl.BlockSpec((None, SUBLANES, tn), lambda l, j: (l, 0, j)),
        out_shape=jax.ShapeDtypeStruct((depth, SUBLANES, n), F32),
        compiler_params=_params("parallel", "parallel"),
        name="adaln_mod",
    )(cond8, w_ada, b_ada.reshape(depth, 1, n))


def _proj_kernel(x_ref, mod_ref, gain_ref, w_ref, out_ref, *, d):
    shift = mod_ref[:, 0:d]
    scale = mod_ref[:, d:2 * d]
    h = _rmsnorm(x_ref[...], gain_ref[...]) * (1.0 + scale) + shift
    out_ref[...] = _dot(h.astype(BF16), w_ref[...])


def _proj_call(x, mod, gain, w, cond_row, tm, name):
    t, d = x.shape
    n = w.shape[1]
    return pl.pallas_call(
        functools.partial(_proj_kernel, d=d),
        grid=(t // tm,),
        in_specs=[
            pl.BlockSpec((tm, d), lambda i: (i, 0)),
            pl.BlockSpec((None, 1, mod.shape[-1]), lambda i: (cond_row(i * tm), 0, 0)),
            _resident((1, d)),
            _resident((d, n)),
        ],
        out_specs=pl.BlockSpec((tm, n), lambda i: (i, 0)),
        out_shape=jax.ShapeDtypeStruct((t, n), F32),
        compiler_params=_params("parallel"),
        name=name,
    )(x, mod, gain, w)


def _hgrn_proj_kernel(x_ref, mod_ref, gain_ref, lb_ref, w_ref, out_ref, *, d):
    shift = mod_ref[:, 0:d]
    scale = mod_ref[:, d:2 * d]
    h = (_rmsnorm(x_ref[...], gain_ref[...]) * (1.0 + scale) + shift).astype(BF16)
    for part in range(5):
        cols = slice(part * d, (part + 1) * d)
        p = _dot(h, w_ref[:, cols])
        if part in (0, 4):
            p = _silu(p)
        elif part in (1, 2):
            lb = lb_ref[:, (part - 1) * d:part * d]
            p = jnp.log(lb + (1.0 - lb) * _sigmoid(p))
        out_ref[:, cols] = p


def _hgrn_proj_call(x, mod, gain, lb, w, cond_row, tm):
    t, d = x.shape
    n = w.shape[1]
    return pl.pallas_call(
        functools.partial(_hgrn_proj_kernel, d=d),
        grid=(t // tm,),
        in_specs=[
            pl.BlockSpec((tm, d), lambda i: (i, 0)),
            pl.BlockSpec((None, 1, mod.shape[-1]), lambda i: (cond_row(i * tm), 0, 0)),
            _resident((1, d)),
            _resident((1, 2 * d)),
            _resident((d, n)),
        ],
        out_specs=pl.BlockSpec((tm, n), lambda i: (i, 0)),
        out_shape=jax.ShapeDtypeStruct((t, n), F32),
        compiler_params=_params("parallel"),
        name="hgrn_in_proj",
    )(x, mod, gain, lb, w)


def _ffn_kernel(x_ref, a0_ref, a1_ref, mod_ref, gain_ref, wo_ref, wgu_ref, wd_ref, nf_ref, *out_refs,
                d, d_ff, ff_chunk, final, tiles0):
    g1 = mod_ref[:, 2 * d:3 * d]
    sh2 = mod_ref[:, 3 * d:4 * d]
    sc2 = mod_ref[:, 4 * d:5 * d]
    g2 = mod_ref[:, 5 * d:6 * d]
    a = jnp.where(pl.program_id(0) < tiles0, a0_ref[...], a1_ref[...])
    x1 = x_ref[...] + g1 * _dot(a.astype(BF16), wo_ref[...])
    h = (_rmsnorm(x1, gain_ref[...]) * (1.0 + sc2) + sh2).astype(BF16)
    y = None
    for j in range(d_ff // ff_chunk):
        lo = j * ff_chunk
        g = _dot(h, wgu_ref[:, lo:lo + ff_chunk])
        u = _dot(h, wgu_ref[:, d_ff + lo:d_ff + lo + ff_chunk])
        part = _dot((_silu(g) * u).astype(BF16), wd_ref[lo:lo + ff_chunk, :])
        y = part if y is None else y + part
    x2 = x1 + g2 * y
    out_refs[0][...] = x2
    if final:
        out_refs[1][...] = _rmsnorm(x2, nf_ref[...])


def _ffn_call(x, a0, a1, mod, gain, w_out, w_gu, w_d, norm_final, cond_row, tm, final):
    t, d = x.shape
    d_ff = w_d.shape[0]
    ff_chunk = d_ff // 2
    tiles0 = a0.shape[0] // tm
    row = pl.BlockSpec((tm, d), lambda i: (i, 0))
    out_shape = [jax.ShapeDtypeStruct((t, d), F32)]
    out_specs = [row]
    if final:
        out_shape.append(jax.ShapeDtypeStruct((t, d), F32))
        out_specs.append(row)
    outs = pl.pallas_call(
        functools.partial(_ffn_kernel, d=d, d_ff=d_ff, ff_chunk=ff_chunk, final=final, tiles0=tiles0),
        grid=(t // tm,),
        in_specs=[
            row,
            pl.BlockSpec((tm, d), lambda i: (jnp.minimum(i, tiles0 - 1), 0)),
            pl.BlockSpec((tm, d), lambda i: (jnp.maximum(i - tiles0, 0), 0)),
            pl.BlockSpec((None, 1, mod.shape[-1]), lambda i: (cond_row(i * tm), 0, 0)),
            _resident((1, d)),
            _resident(w_out.shape),
            _resident(w_gu.shape),
            _resident(w_d.shape),
            _resident((1, d)),
        ],
        out_specs=out_specs,
        out_shape=out_shape,
        compiler_params=_params("parallel"),
        name="outproj_ffn",
    )(x, a0, a1, mod, gain, w_out, w_gu, w_d, norm_final)
    return outs


def _row_blocks(b, rows, size):
    parts = [jnp.broadcast_to(b[r:r + 1, :], (size, b.shape[1])) for r in rows]
    return parts[0] if len(parts) == 1 else jnp.concatenate(parts, axis=0)


def _scores_blockwise(q, k, b, reverse):
    c, dk = q.shape
    blk = GLA_BLOCK
    nblk = c // blk
    zero_row = jnp.zeros((1, dk), F32)
    rows = []
    for i in range(nblk):
        lo, hi = i * blk, (i + 1) * blk
        if reverse:
            r = b[hi:hi + 1, :] if i < nblk - 1 else zero_row
            key_lo, key_hi = lo, c
        else:
            r = b[lo - 1:lo, :] if i > 0 else zero_row
            key_lo, key_hi = 0, hi
        qi = (q[lo:hi] * jnp.exp(b[lo:hi] - r)).astype(BF16)
        ki = (k[key_lo:key_hi] * jnp.exp(r - b[key_lo:key_hi])).astype(BF16)
        if key_hi - key_lo < c:
            pad = jnp.zeros((c - (key_hi - key_lo), dk), BF16)
            ki = jnp.concatenate([pad, ki] if reverse else [ki, pad], axis=0)
        rows.append(_dot_nt(qi, ki))
    return jnp.concatenate(rows, axis=0)


def _causal(c, reverse):
    t_idx = lax.broadcasted_iota(jnp.int32, (c, c), 0)
    s_idx = lax.broadcasted_iota(jnp.int32, (c, c), 1)
    return (s_idx >= t_idx) if reverse else (s_idx <= t_idx)


def _gla_decay_sums(lg, tri):
    hi = lg.astype(BF16)
    mid = (lg - hi.astype(F32)).astype(BF16)
    return _dot(jnp.where(tri, 1.0, 0.0).astype(BF16), jnp.concatenate([hi, mid], axis=1))


def _gla_mix(q, lg, b2, v, st, tri, reverse, bounded):
    c = q.shape[0]
    dk = q.shape[1]
    k = 1.0 - jnp.exp(lg)
    b = b2[:, 0:dk] + b2[:, dk:2 * dk]
    b_end = b[0:1, :] if reverse else b[c - 1:c, :]
    o = _dot_nt((q * jnp.exp(b)).astype(BF16), st.astype(BF16))
    k_dec = (k * jnp.exp(b_end - b)).astype(BF16)
    st_new = st * jnp.exp(b_end) + _dot(v.T.astype(BF16), k_dec)
    if bounded:
        scores = jnp.where(tri, _scores_blockwise(q, k, b, reverse), 0.0)
    else:
        scores = _scores_robust(q, k, b, tri, reverse)
    return o, st_new, scores.astype(BF16)


def _scores_robust(q, k, b, tri, reverse):
    c, dk = q.shape
    t_idx = lax.broadcasted_iota(jnp.int32, (c, c), 0)
    s_idx = lax.broadcasted_iota(jnp.int32, (c, c), 1)
    pos = lax.broadcasted_iota(jnp.int32, (c, dk), 0)
    scores = jnp.zeros((c, c), F32)
    m = c // 2
    while m >= GLA_LEAF:
        nblk = c // (2 * m)
        bound = [kb * 2 * m + (m if reverse else m - 1) for kb in range(nblk)]
        e = jnp.exp(-jnp.abs(b - _row_blocks(b, bound, 2 * m)))
        late = _mod(pos, 2 * m) >= m
        q_side = jnp.logical_not(late) if reverse else late
        ql = jnp.where(q_side, q * e, 0.0).astype(BF16)
        kl = jnp.where(q_side, 0.0, k * e).astype(BF16)
        s_l = _dot_nt(ql, kl)
        if nblk > 1:
            s_l = jnp.where(_div(t_idx, 2 * m) == _div(s_idx, 2 * m), s_l, 0.0)
        scores = scores + s_l
        m //= 2
    nleaf = c // GLA_LEAF
    k3 = k.reshape(nleaf, GLA_LEAF, dk)
    b3 = b.reshape(nleaf, GLA_LEAF, dk)
    sel_d = _mod(lax.broadcasted_iota(jnp.int32, (dk, c), 1), GLA_LEAF)
    r = jnp.zeros((c, c), F32)
    for sl in range(GLA_LEAF):
        ks = jnp.broadcast_to(k3[:, sl:sl + 1, :], k3.shape).reshape(c, dk)
        bs = jnp.broadcast_to(b3[:, sl:sl + 1, :], b3.shape).reshape(c, dk)
        term = (q * ks * jnp.exp(jnp.minimum(b - bs, 0.0))).astype(BF16)
        r = r + _dot(term, jnp.where(sel_d == sl, 1.0, 0.0).astype(BF16))
    same_leaf = _div(t_idx, GLA_LEAF) == _div(s_idx, GLA_LEAF)
    return scores + jnp.where(jnp.logical_and(same_leaf, tri), r, 0.0)


def _gla_kernel(*refs, n, hb, has_s0, want_state):
    q_ref, lgf_ref, lgb_ref, v_ref, g_ref, gn_ref = refs[:6]
    pos = 6
    s0_ref = None
    if has_s0:
        s0_ref = refs[pos]
        pos += 1
    o_ref = refs[pos]
    pos += 1
    sfin_ref = None
    if want_state:
        sfin_ref = refs[pos]
        pos += 1
    of_ref, ob_ref, st_ref = refs[pos:pos + 3]
    c = GLA_CHUNK
    hd = LANES
    nchunks = n // c

    for direction in (0, 1):
        for h in range(hb):
            if has_s0:
                st_ref[direction, h] = s0_ref[direction, h].T
            else:
                st_ref[direction, h] = jnp.zeros((hd, hd), F32)

    def block_decay(i, low):
        rows = pl.ds(pl.multiple_of(i * c, c), c)
        for lg_ref in (lgf_ref, lgb_ref):
            for h in range(hb):
                lg = lg_ref[rows, h * hd:(h + 1) * hd].reshape(c // GLA_BLOCK, GLA_BLOCK, hd)
                low = jnp.minimum(low, jnp.min(jnp.sum(lg, axis=1), axis=0, keepdims=True))
        return low

    low = lax.fori_loop(0, nchunks, block_decay, jnp.zeros((1, hd), F32))
    bounded = jnp.min(low) >= -GLA_RANGE_BOUND

    def scan(is_bounded):
        def body(i, carry):
            tri = (_causal(c, False), _causal(c, True))
            chains = []
            for h in range(hb):
                cols = slice(h * hd, (h + 1) * hd)
                for direction in (0, 1):
                    ci = i if direction == 0 else nchunks - 1 - i
                    rows = pl.ds(pl.multiple_of(ci * c, c), c)
                    lg = (lgf_ref if direction == 0 else lgb_ref)[rows, cols]
                    chains.append((h, direction, rows, cols, lg, _gla_decay_sums(lg, tri[direction])))
            mixed = []
            for h, direction, rows, cols, lg, b2 in chains:
                mixed.append(_gla_mix(q_ref[rows, cols], lg, b2, v_ref[rows, cols], st_ref[direction, h],
                                      tri[direction], direction == 1, is_bounded))
            for (h, direction, rows, cols, _, _), (o, st_new, scores) in zip(chains, mixed):
                st_ref[direction, h] = st_new
                o = o + _dot(scores, v_ref[rows, cols].astype(BF16))
                (of_ref if direction == 0 else ob_ref)[rows, cols] = o
            return carry

        lax.fori_loop(0, nchunks, body, 0)

    pl.when(bounded)(functools.partial(scan, True))
    pl.when(jnp.logical_not(bounded))(functools.partial(scan, False))

    def finish(i, carry):
        rows = pl.ds(pl.multiple_of(i * c, c), c)
        for h in range(hb):
            cols = slice(h * hd, (h + 1) * hd)
            o = of_ref[rows, cols] + ob_ref[rows, cols]
            o = o * lax.rsqrt(jnp.mean(o * o, axis=-1, keepdims=True) + EPS) * gn_ref[:, cols]
            o_ref[rows, cols] = o * g_ref[rows, cols]
        return carry

    lax.fori_loop(0, nchunks, finish, 0)
    if want_state:
        for direction in (0, 1):
            for h in range(hb):
                sfin_ref[direction, h] = st_ref[direction, h].T


def _gla_call(proj, gnorm, s0, row0, nseq, n, hb, want_state, name):
    hd = LANES
    heads = A_HEADS
    hblocks = heads // hb
    blk0 = row0 // n

    def col(kind):
        return pl.BlockSpec((n, hb * hd), lambda b, h: (blk0 + b, kind * hblocks + h))

    in_specs = [col(0), col(1), col(2), col(3), col(4),
                pl.BlockSpec((1, hb * hd), lambda b, h: (0, h))]
    args = [proj, proj, proj, proj, proj, gnorm]
    state_spec = pl.BlockSpec((None, 2, hb, hd, hd), lambda b, h: (b, 0, h, 0, 0))
    if s0 is not None:
        in_specs.append(state_spec)
        args.append(s0)
    out_shape = [jax.ShapeDtypeStruct((nseq * n, heads * hd), F32)]
    out_specs = [pl.BlockSpec((n, hb * hd), lambda b, h: (b, h))]
    if want_state:
        out_shape.append(jax.ShapeDtypeStruct((nseq, 2, heads, hd, hd), F32))
        out_specs.append(state_spec)
    return pl.pallas_call(
        functools.partial(_gla_kernel, n=n, hb=hb, has_s0=s0 is not None, want_state=want_state),
        grid=(nseq, hblocks),
        in_specs=in_specs,
        out_specs=out_specs,
        out_shape=out_shape,
        scratch_shapes=[pltpu.VMEM((n, hb * hd), F32), pltpu.VMEM((n, hb * hd), F32),
                        pltpu.VMEM((2, hb, hd, hd), F32)],
        compiler_params=_params("parallel", "parallel"),
        name=name,
    )(*args)


def _stack_groups(ref):
    return jnp.concatenate([ref[:, g * LANES:(g + 1) * LANES] for g in range(GROUP)], axis=0)


def _sink_column(sink_ref, kh, rows):
    gi = _div(lax.broadcasted_iota(jnp.int32, (GROUP * rows, 1), 0), rows)
    col = jnp.zeros((GROUP * rows, 1), F32)
    for g in range(GROUP):
        col = jnp.where(gi == g, sink_ref[kh, g], col)
    return col


def _attn_ctx_kernel(sink_ref, q_ref, k_ref, v_ref, o_ref, *, scale):
    kh = pl.program_id(1)
    rows = q_ref.shape[0]
    q = _stack_groups(q_ref).astype(BF16)
    lg = _dot_nt(q, k_ref[...].astype(BF16)) * scale
    sink = _sink_column(sink_ref, kh, rows)
    mx = jnp.maximum(jnp.max(lg, axis=-1, keepdims=True), sink)
    p = jnp.exp(lg - mx)
    den = jnp.sum(p, axis=-1, keepdims=True) + jnp.exp(sink - mx)
    o = _dot((p / den).astype(BF16), v_ref[...].astype(BF16))
    for g in range(GROUP):
        o_ref[:, g * LANES:(g + 1) * LANES] = o[g * rows:(g + 1) * rows, :]


def _attn_ctx_call(qkv, sink, nseq, n):
    hd = LANES
    kcol = ATT_HEADS
    vcol = ATT_HEADS + KV_HEADS
    return pl.pallas_call(
        functools.partial(_attn_ctx_kernel, scale=hd ** -0.5),
        grid=(nseq, KV_HEADS),
        in_specs=[
            pl.BlockSpec(memory_space=pltpu.SMEM),
            pl.BlockSpec((n, GROUP * hd), lambda b, kh: (b, kh)),
            pl.BlockSpec((n, hd), lambda b, kh: (b, kcol + kh)),
            pl.BlockSpec((n, hd), lambda b, kh: (b, vcol + kh)),
        ],
        out_specs=pl.BlockSpec((n, GROUP * hd), lambda b, kh: (b, kh)),
        out_shape=jax.ShapeDtypeStruct((nseq * n, ATT_HEADS * hd), F32),
        compiler_params=_params("parallel", "parallel"),
        name="attn_context",
    )(sink, qkv, qkv, qkv)


def _rope_kernel(x_ref, cos_ref, sa_ref, sb_ref, o_ref):
    cos = cos_ref[...]
    sa = sa_ref[...]
    sb = sb_ref[...]
    for h in range(x_ref.shape[1] // LANES):
        x = x_ref[:, h * LANES:(h + 1) * LANES]
        o_ref[:, h * LANES:(h + 1) * LANES] = (
            x * cos + pltpu.roll(x, LANES - 32, 1) * sa + pltpu.roll(x, 32, 1) * sb)


def _rope_call(qkv, tables, row0, nrows, n, tm):
    width = (ATT_HEADS + KV_HEADS) * LANES
    tab = pl.BlockSpec((tm, LANES), lambda i: (i % (n // tm), 0))
    return pl.pallas_call(
        _rope_kernel,
        grid=(nrows // tm,),
        in_specs=[pl.BlockSpec((tm, width), lambda i: (row0 // tm + i, 0)), tab, tab, tab],
        out_specs=pl.BlockSpec((tm, width), lambda i: (i, 0)),
        out_shape=jax.ShapeDtypeStruct((nrows, width), F32),
        compiler_params=_params("parallel"),
        name="axial_rope",
    )(qkv, *tables)


def _attn_lat_kernel(sink_ref, q_ref, k0_ref, k1_ref, k2_ref, v0_ref, v1_ref, v2_ref, kc_ref, vc_ref, o_ref,
                     *, scale, n):
    nb = pl.program_id(1)
    kh = pl.program_id(2)
    rows = q_ref.shape[0]
    q = _stack_groups(q_ref).astype(BF16)
    kw = jnp.concatenate([k0_ref[...], k1_ref[...], k2_ref[...]], axis=0).astype(BF16)
    vw = jnp.concatenate([v0_ref[...], v1_ref[...], v2_ref[...]], axis=0).astype(BF16)
    lc = _dot_nt(q, kc_ref[...].astype(BF16)) * scale
    lw = _dot_nt(q, kw) * scale
    qi = _mod(lax.broadcasted_iota(jnp.int32, lw.shape, 0), rows)
    kj = lax.broadcasted_iota(jnp.int32, lw.shape, 1)
    rel = kj - BLOCK - qi
    key_pos = (nb - 1) * BLOCK + kj
    valid = (jnp.abs(rel) <= WINDOW) & (key_pos >= 0) & (key_pos < n)
    lw = jnp.where(valid, lw, NEG_BIG)
    sink = _sink_column(sink_ref, kh, rows)
    mx = jnp.maximum(jnp.maximum(jnp.max(lc, axis=-1, keepdims=True), jnp.max(lw, axis=-1, keepdims=True)), sink)
    pc = jnp.exp(lc - mx)
    pw = jnp.where(valid, jnp.exp(lw - mx), 0.0)
    den = jnp.sum(pc, axis=-1, keepdims=True) + jnp.sum(pw, axis=-1, keepdims=True) + jnp.exp(sink - mx)
    o = _dot((pc / den).astype(BF16), vc_ref[...].astype(BF16)) + _dot((pw / den).astype(BF16), vw)
    for g in range(GROUP):
        o_ref[:, g * LANES:(g + 1) * LANES] = o[g * rows:(g + 1) * rows, :]


def _attn_lat_call(qk_rope, qkv, cache_k, cache_v, sink, row0, nseq, n):
    hd = LANES
    nblk = n // BLOCK
    past = cache_k.shape[1]
    vcol = ATT_HEADS + KV_HEADS
    vrow0 = row0 // BLOCK

    def near(off, base, colbase):
        def index(b, i, kh):
            return (base + b * nblk + jnp.clip(i + off, 0, nblk - 1), colbase + kh)
        return pl.BlockSpec((BLOCK, hd), index)

    cache_spec = pl.BlockSpec((None, past, hd), lambda b, i, kh: (b, 0, kh))
    return pl.pallas_call(
        functools.partial(_attn_lat_kernel, scale=hd ** -0.5, n=n),
        grid=(nseq, nblk, KV_HEADS),
        in_specs=[
            pl.BlockSpec(memory_space=pltpu.SMEM),
            pl.BlockSpec((BLOCK, GROUP * hd), lambda b, i, kh: (b * nblk + i, kh)),
            near(-1, 0, ATT_HEADS), near(0, 0, ATT_HEADS), near(1, 0, ATT_HEADS),
            near(-1, vrow0, vcol), near(0, vrow0, vcol), near(1, vrow0, vcol),
            cache_spec, cache_spec,
        ],
        out_specs=pl.BlockSpec((BLOCK, GROUP * hd), lambda b, i, kh: (b * nblk + i, kh)),
        out_shape=jax.ShapeDtypeStruct((nseq * n, ATT_HEADS * hd), F32),
        compiler_params=_params("parallel", "parallel", "parallel"),
        name="attn_latent",
    )(sink, qk_rope, qk_rope, qk_rope, qk_rope, qkv, qkv, qkv, cache_k, cache_v)


def _rope_tables(n):
    quarter = LANES // 4
    inv = ROPE_BASE ** (-jnp.arange(quarter, dtype=F32) / quarter)
    pos = jnp.arange(n)
    row_pos = (pos // GRID_W).astype(F32)
    col_pos = (pos % GRID_W).astype(F32)
    ang_r = row_pos[:, None] * inv[None, :]
    ang_c = col_pos[:, None] * inv[None, :]
    zero = jnp.zeros_like(ang_r)
    cos = jnp.concatenate([jnp.cos(ang_r), jnp.cos(ang_r), jnp.cos(ang_c), jnp.cos(ang_c)], axis=1)
    sin_up = jnp.concatenate([-jnp.sin(ang_r), zero, -jnp.sin(ang_c), zero], axis=1)
    sin_dn = jnp.concatenate([zero, jnp.sin(ang_r), zero, jnp.sin(ang_c)], axis=1)
    return cos, sin_up, sin_dn


def _lower_bounds(lb_param):
    p = jax.nn.softmax(lb_param.astype(F32), axis=0)
    cs = jnp.cumsum(p, axis=0)
    return cs - cs[:1]


def kernel(x_prompt, x_sample, cache_k, cache_v, state_hgrn, c, c_ctx, w_ada, b_ada, norm1, norm2, norm_final,
           w_gate_up, w_down, w_in_a, lower_bounds, gnorm_a, w_out_a, w_qkv_b, w_out_b, sink_b):
    batch, seq, d = x_prompt.shape
    dec_batch, dec_seq, _ = x_sample.shape
    depth = w_ada.shape[0]
    t_ctx = batch * seq
    t_lat = dec_batch * dec_seq
    assert dec_batch + 1 <= SUBLANES and dec_seq % seq == 0

    def cond_row(tok0):
        return jnp.where(tok0 < t_ctx, 0, 1 + (tok0 - t_ctx) // dec_seq)

    x = jnp.concatenate([x_prompt.reshape(t_ctx, d), x_sample.reshape(t_lat, d)], axis=0)
    cond8 = jnp.concatenate([c_ctx[None, :], c, jnp.zeros((SUBLANES - 1 - dec_batch, d), F32)], axis=0)
    mod = _ada_call(cond8, w_ada, b_ada).reshape(depth, SUBLANES, 1, 6 * d)
    lb_all = _lower_bounds(lower_bounds)
    rope_tabs = _rope_tables(dec_seq)
    past = cache_k.shape[2]
    nf = norm_final.reshape(1, d)

    new_k, new_v, new_s = [], [], []
    y = None
    for l in range(depth):
        j = l // 2
        n1 = norm1[l].reshape(1, d)
        n2 = norm2[l].reshape(1, d)
        if l % 2 == 0:
            proj = _hgrn_proj_call(x, mod[l], n1, lb_all[j].reshape(1, 2 * d), w_in_a[j].astype(BF16),
                                   cond_row, seq)
            gn = gnorm_a[j].reshape(1, d)
            a_ctx, s_fin = _gla_call(proj, gn, None, 0, batch, seq, GLA_HEADS_PER_STEP, True, "hgrn_scan_ctx")
            (a_lat,) = _gla_call(proj, gn, state_hgrn[:, j], t_ctx, dec_batch, dec_seq,
                                 GLA_HEADS_PER_STEP, False, "hgrn_scan_lat")
            new_s.append(s_fin)
            w_mix = w_out_a[j]
        else:
            qkv = _proj_call(x, mod[l], n1, w_qkv_b[j].astype(BF16), cond_row, seq, "attn_qkv_proj")
            sink = sink_b[j].reshape(KV_HEADS, GROUP)
            kv = qkv[:t_ctx, ATT_HEADS * LANES:].reshape(batch, seq, 2, KV_HEADS, LANES)
            new_k.append(kv[:, :, 0])
            new_v.append(kv[:, :, 1])
            a_ctx = _attn_ctx_call(qkv, sink, batch, seq)
            qk_rope = _rope_call(qkv, rope_tabs, t_ctx, t_lat, dec_seq, 512)
            a_lat = _attn_lat_call(qk_rope, qkv, cache_k[:, j].reshape(dec_batch, past, KV_HEADS * LANES),
                                   cache_v[:, j].reshape(dec_batch, past, KV_HEADS * LANES), sink,
                                   t_ctx, dec_batch, dec_seq)
            w_mix = w_out_b[j]
        final = l == depth - 1
        outs = _ffn_call(x, a_ctx, a_lat, mod[l], n2, w_mix.astype(BF16), w_gate_up[l].astype(BF16),
                         w_down[l].astype(BF16), nf, cond_row, 512, final)
        x = outs[0]
        if final:
            y = outs[1]

    y_prompt = y[:t_ctx].reshape(batch, seq, d)
    y_sample = y[t_ctx:].reshape(dec_batch, dec_seq, d)
    return (y_prompt, y_sample, jnp.stack(new_k, axis=1), jnp.stack(new_v, axis=1), jnp.stack(new_s, axis=1))
```

```python
import functools

import jax
import jax.numpy as jnp
from jax import lax
from jax.experimental import pallas as pl
from jax.experimental.pallas import tpu as pltpu

F32 = jnp.float32
BF16 = jnp.bfloat16
EPS = 1e-6
ROPE_BASE = 10000.0

V7X_VMEM_LIMIT_BYTES = 56 * 1024 * 1024
SUBLANES = 8
LANES = 128

A_HEADS = 8
ATT_HEADS = 8
KV_HEADS = 2
GROUP = ATT_HEADS // KV_HEADS
WINDOW = 128
BLOCK = 128
GRID_W = 64
TOKEN_TILE = 512
HGRN_PROJ_TILE = 256
GLA_CHUNK = 128
GLA_LEAF = SUBLANES
GLA_HEADS_CTX = 4
GLA_HEADS_LAT = 2
GLA_BLOCK = 32
GLA_RANGE_BOUND = 60.0
NEG_BIG = -1e30


def _params(*sem):
    return pltpu.CompilerParams(dimension_semantics=sem, vmem_limit_bytes=V7X_VMEM_LIMIT_BYTES)


def _resident(shape):
    nd = len(shape)
    return pl.BlockSpec(shape, lambda *_: (0,) * nd, pipeline_mode=pl.Buffered(1))


def _div(x, k):
    assert k & (k - 1) == 0
    return jnp.right_shift(x, k.bit_length() - 1)


def _mod(x, k):
    assert k & (k - 1) == 0
    return jnp.bitwise_and(x, k - 1)


def _sigmoid(x):
    return 1.0 / (1.0 + jnp.exp(-x))


def _silu(x):
    return x * _sigmoid(x)


def _rmsnorm(x, gain):
    return x * lax.rsqrt(jnp.mean(x * x, axis=-1, keepdims=True) + EPS) * gain


def _dot(a, b):
    return jnp.dot(a, b, preferred_element_type=F32)


def _dot_nt(a, b):
    return lax.dot_general(a, b, (((1,), (1,)), ((), ())), preferred_element_type=F32)


def _ada_kernel(cond_ref, w_ref, b_ref, out_ref):
    s = _silu(cond_ref[...]).astype(BF16)
    out_ref[...] = _dot(s, w_ref[...].astype(BF16)) + b_ref[...]


def _ada_call(cond8, w_ada, b_ada):
    depth, d, n = w_ada.shape
    tn = n // 4
    return pl.pallas_call(
        _ada_kernel,
        grid=(depth, n // tn),
        in_specs=[
            pl.BlockSpec((SUBLANES, d), lambda l, j: (0, 0)),
            pl.BlockSpec((None, d, tn), lambda l, j: (l, 0, j)),
            pl.BlockSpec((None, 1, tn), lambda l, j: (l, 0, j)),
        ],
        out_specs=pl.BlockSpec((None, SUBLANES, tn), lambda l, j: (l, 0, j)),
        out_shape=jax.ShapeDtypeStruct((depth, SUBLANES, n), F32),
        compiler_params=_params("parallel", "parallel"),
        name="adaln_mod",
    )(cond8, w_ada, b_ada.reshape(depth, 1, n))


def _split_specs(tm, width, tiles0):
    return [pl.BlockSpec((tm, width), lambda i: (jnp.minimum(i, tiles0 - 1), 0)),
            pl.BlockSpec((tm, width), lambda i: (jnp.maximum(i - tiles0, 0), 0))]


def _load_split(ref0, ref1, tiles0):
    return jnp.where(pl.program_id(0) < tiles0, ref0[...], ref1[...])


def _store_split(ref0, ref1, tiles0, value):
    @pl.when(pl.program_id(0) < tiles0)
    def _():
        ref0[...] = value

    @pl.when(pl.program_id(0) >= tiles0)
    def _():
        ref1[...] = value


def _modulated(x0_ref, x1_ref, mod_ref, gain_ref, tiles0, d):
    shift = mod_ref[:, 0:d]
    scale = mod_ref[:, d:2 * d]
    x = _load_split(x0_ref, x1_ref, tiles0)
    return (_rmsnorm(x, gain_ref[...]) * (1.0 + scale) + shift).astype(BF16)


def _proj_kernel(x0_ref, x1_ref, mod_ref, gain_ref, w_ref, out_ref, *, d, tiles0):
    out_ref[...] = _dot(_modulated(x0_ref, x1_ref, mod_ref, gain_ref, tiles0, d), w_ref[...])


def _hgrn_proj_kernel(x0_ref, x1_ref, mod_ref, gain_ref, lb_ref, w_ref, out_ref, *, d, tiles0):
    h = _modulated(x0_ref, x1_ref, mod_ref, gain_ref, tiles0, d)
    for part in range(5):
        cols = slice(part * d, (part + 1) * d)
        p = _dot(h, w_ref[:, cols])
        if part in (0, 4):
            p = _silu(p)
        elif part in (1, 2):
            lb = lb_ref[:, (part - 1) * d:part * d]
            p = jnp.log(lb + (1.0 - lb) * _sigmoid(p))
        out_ref[:, cols] = p


def _proj_call(x, mod, gain, w, cond_row, tm, lb=None):
    x0, x1 = x
    d = x0.shape[1]
    t = x0.shape[0] + x1.shape[0]
    n = w.shape[1]
    tiles0 = x0.shape[0] // tm
    hgrn = lb is not None
    extra_specs = [_resident((1, 2 * d))] if hgrn else []
    extra_args = [lb] if hgrn else []
    return pl.pallas_call(
        functools.partial(_hgrn_proj_kernel if hgrn else _proj_kernel, d=d, tiles0=tiles0),
        grid=(t // tm,),
        in_specs=_split_specs(tm, d, tiles0) + [
            pl.BlockSpec((None, 1, mod.shape[-1]), lambda i: (cond_row(i * tm), 0, 0)),
            _resident((1, d)),
        ] + extra_specs + [_resident((d, n))],
        out_specs=pl.BlockSpec((tm, n), lambda i: (i, 0)),
        out_shape=jax.ShapeDtypeStruct((t, n), F32),
        compiler_params=_params("parallel"),
        name="hgrn_in_proj" if hgrn else "attn_qkv_proj",
    )(x0, x1, mod, gain, *extra_args, w)


def _ffn_kernel(x0_ref, x1_ref, a0_ref, a1_ref, mod_ref, gain_ref, wo_ref, wgu_ref, wd_ref, nf_ref, *out_refs,
                d, d_ff, ff_chunk, final, tiles0):
    g1 = mod_ref[:, 2 * d:3 * d]
    sh2 = mod_ref[:, 3 * d:4 * d]
    sc2 = mod_ref[:, 4 * d:5 * d]
    g2 = mod_ref[:, 5 * d:6 * d]
    a = _load_split(a0_ref, a1_ref, tiles0)
    x1 = _load_split(x0_ref, x1_ref, tiles0) + g1 * _dot(a.astype(BF16), wo_ref[...])
    h = (_rmsnorm(x1, gain_ref[...]) * (1.0 + sc2) + sh2).astype(BF16)
    y = None
    for j in range(d_ff // ff_chunk):
        lo = j * ff_chunk
        g = _dot(h, wgu_ref[:, lo:lo + ff_chunk])
        u = _dot(h, wgu_ref[:, d_ff + lo:d_ff + lo + ff_chunk])
        part = _dot((_silu(g) * u).astype(BF16), wd_ref[lo:lo + ff_chunk, :])
        y = part if y is None else y + part
    x2 = x1 + g2 * y
    if final:
        x2 = _rmsnorm(x2, nf_ref[...])
    _store_split(out_refs[0], out_refs[1], tiles0, x2)


def _ffn_call(x, a, mod, gain, w_out, w_gu, w_d, norm_final, cond_row, tm, final):
    (x0, x1), (a0, a1) = x, a
    d = x0.shape[1]
    d_ff = w_d.shape[0]
    ff_chunk = d_ff
    tiles0 = x0.shape[0] // tm
    rows = _split_specs(tm, d, tiles0)
    return pl.pallas_call(
        functools.partial(_ffn_kernel, d=d, d_ff=d_ff, ff_chunk=ff_chunk, final=final, tiles0=tiles0),
        grid=((x0.shape[0] + x1.shape[0]) // tm,),
        in_specs=rows + rows + [
            pl.BlockSpec((None, 1, mod.shape[-1]), lambda i: (cond_row(i * tm), 0, 0)),
            _resident((1, d)),
            _resident(w_out.shape),
            _resident(w_gu.shape),
            _resident(w_d.shape),
            _resident((1, d)),
        ],
        out_specs=rows,
        out_shape=[jax.ShapeDtypeStruct(x0.shape, F32), jax.ShapeDtypeStruct(x1.shape, F32)],
        compiler_params=_params("arbitrary"),
        name="outproj_ffn",
    )(x0, x1, a0, a1, mod, gain, w_out, w_gu, w_d, norm_final)


def _row_blocks(b, rows, size):
    parts = [jnp.broadcast_to(b[r:r + 1, :], (size, b.shape[1])) for r in rows]
    return parts[0] if len(parts) == 1 else jnp.concatenate(parts, axis=0)


def _scores_blockwise(q, k, b, reverse):
    c, dk = q.shape
    blk = GLA_BLOCK
    nblk = c // blk
    zero_row = jnp.zeros((1, dk), F32)
    rows = []
    for i in range(nblk):
        lo, hi = i * blk, (i + 1) * blk
        if reverse:
            r = b[hi:hi + 1, :] if i < nblk - 1 else zero_row
            key_lo, key_hi = lo, c
        else:
            r = b[lo - 1:lo, :] if i > 0 else zero_row
            key_lo, key_hi = 0, hi
        qi = (q[lo:hi] * jnp.exp(b[lo:hi] - r)).astype(BF16)
        ki = (k[key_lo:key_hi] * jnp.exp(r - b[key_lo:key_hi])).astype(BF16)
        if key_hi - key_lo < c:
            pad = jnp.zeros((c - (key_hi - key_lo), dk), BF16)
            ki = jnp.concatenate([pad, ki] if reverse else [ki, pad], axis=0)
        rows.append(_dot_nt(qi, ki))
    return jnp.concatenate(rows, axis=0)


def _causal(c, reverse):
    t_idx = lax.broadcasted_iota(jnp.int32, (c, c), 0)
    s_idx = lax.broadcasted_iota(jnp.int32, (c, c), 1)
    return (s_idx >= t_idx) if reverse else (s_idx <= t_idx)


def _gla_decay_sums(lg, tri):
    hi = lg.astype(BF16)
    mid = (lg - hi.astype(F32)).astype(BF16)
    return _dot(jnp.where(tri, 1.0, 0.0).astype(BF16), jnp.concatenate([hi, mid], axis=1))


def _gla_mix(q, lg, b2, v, st, tri, reverse, bounded):
    c = q.shape[0]
    dk = q.shape[1]
    k = 1.0 - jnp.exp(lg)
    b = b2[:, 0:dk] + b2[:, dk:2 * dk]
    b_end = b[0:1, :] if reverse else b[c - 1:c, :]
    o = _dot_nt((q * jnp.exp(b)).astype(BF16), st.astype(BF16))
    k_dec = (k * jnp.exp(b_end - b)).astype(BF16)
    st_new = st * jnp.exp(b_end) + _dot(v.T.astype(BF16), k_dec)
    if bounded:
        scores = jnp.where(tri, _scores_blockwise(q, k, b, reverse), 0.0)
    else:
        scores = _scores_robust(q, k, b, tri, reverse)
    return o, st_new, scores.astype(BF16)


def _scores_robust(q, k, b, tri, reverse):
    c, dk = q.shape
    t_idx = lax.broadcasted_iota(jnp.int32, (c, c), 0)
    s_idx = lax.broadcasted_iota(jnp.int32, (c, c), 1)
    pos = lax.broadcasted_iota(jnp.int32, (c, dk), 0)
    scores = jnp.zeros((c, c), F32)
    m = c // 2
    while m >= GLA_LEAF:
        nblk = c // (2 * m)
        bound = [kb * 2 * m + (m if reverse else m - 1) for kb in range(nblk)]
        e = jnp.exp(-jnp.abs(b - _row_blocks(b, bound, 2 * m)))
        late = _mod(pos, 2 * m) >= m
        q_side = jnp.logical_not(late) if reverse else late
        ql = jnp.where(q_side, q * e, 0.0).astype(BF16)
        kl = jnp.where(q_side, 0.0, k * e).astype(BF16)
        s_l = _dot_nt(ql, kl)
        if nblk > 1:
            s_l = jnp.where(_div(t_idx, 2 * m) == _div(s_idx, 2 * m), s_l, 0.0)
        scores = scores + s_l
        m //= 2
    nleaf = c // GLA_LEAF
    k3 = k.reshape(nleaf, GLA_LEAF, dk)
    b3 = b.reshape(nleaf, GLA_LEAF, dk)
    sel_d = _mod(lax.broadcasted_iota(jnp.int32, (dk, c), 1), GLA_LEAF)
    r = jnp.zeros((c, c), F32)
    for sl in range(GLA_LEAF):
        ks = jnp.broadcast_to(k3[:, sl:sl + 1, :], k3.shape).reshape(c, dk)
        bs = jnp.broadcast_to(b3[:, sl:sl + 1, :], b3.shape).reshape(c, dk)
        term = (q * ks * jnp.exp(jnp.minimum(b - bs, 0.0))).astype(BF16)
        r = r + _dot(term, jnp.where(sel_d == sl, 1.0, 0.0).astype(BF16))
    same_leaf = _div(t_idx, GLA_LEAF) == _div(s_idx, GLA_LEAF)
    return scores + jnp.where(jnp.logical_and(same_leaf, tri), r, 0.0)


def _gla_kernel(*refs, n, hb, has_s0, nprev, want_state):
    q_ref, lgf_ref, lgb_ref, v_ref, g_ref, gn_ref = refs[:6]
    pos = 6
    s0_ref = prev_ref = sfin_ref = None
    if has_s0:
        s0_ref = refs[pos]
        pos += 1
    if nprev:
        prev_ref = refs[pos]
        pos += 1
    o_ref = refs[pos]
    pos += 1
    if want_state:
        sfin_ref = refs[pos]
        pos += 1
    of_ref, ob_ref, st_ref = refs[pos:pos + 3]
    c = GLA_CHUNK
    hd = LANES
    nchunks = n // c

    for direction in (0, 1):
        for h in range(hb):
            if has_s0:
                st_ref[direction, h] = s0_ref[direction, h].T
            else:
                st_ref[direction, h] = jnp.zeros((hd, hd), F32)

    def block_decay(i, low):
        rows = pl.ds(pl.multiple_of(i * c, c), c)
        for lg_ref in (lgf_ref, lgb_ref):
            for h in range(hb):
                lg = lg_ref[rows, h * hd:(h + 1) * hd].reshape(c // GLA_BLOCK, GLA_BLOCK, hd)
                low = jnp.minimum(low, jnp.min(jnp.sum(lg, axis=1), axis=0, keepdims=True))
        return low

    low = lax.fori_loop(0, nchunks, block_decay, jnp.zeros((1, hd), F32))
    bounded = jnp.min(low) >= -GLA_RANGE_BOUND

    def scan(is_bounded):
        def body(i, carry):
            tri = (_causal(c, False), _causal(c, True))
            chains = []
            for h in range(hb):
                cols = slice(h * hd, (h + 1) * hd)
                for direction in (0, 1):
                    ci = i if direction == 0 else nchunks - 1 - i
                    rows = pl.ds(pl.multiple_of(ci * c, c), c)
                    lg = (lgf_ref if direction == 0 else lgb_ref)[rows, cols]
                    chains.append((h, direction, rows, cols, lg, _gla_decay_sums(lg, tri[direction])))
            mixed = []
            for h, direction, rows, cols, lg, b2 in chains:
                mixed.append(_gla_mix(q_ref[rows, cols], lg, b2, v_ref[rows, cols], st_ref[direction, h],
                                      tri[direction], direction == 1, is_bounded))
            for (h, direction, rows, cols, _, _), (o, st_new, scores) in zip(chains, mixed):
                st_ref[direction, h] = st_new
                o = o + _dot(scores, v_ref[rows, cols].astype(BF16))
                (of_ref if direction == 0 else ob_ref)[rows, cols] = o
            return carry

        lax.fori_loop(0, nchunks, body, 0)

    pl.when(bounded)(functools.partial(scan, True))
    pl.when(jnp.logical_not(bounded))(functools.partial(scan, False))

    def finish(i, carry):
        rows = pl.ds(pl.multiple_of(i * c, c), c)
        for h in range(hb):
            cols = slice(h * hd, (h + 1) * hd)
            o = of_ref[rows, cols] + ob_ref[rows, cols]
            o = o * lax.rsqrt(jnp.mean(o * o, axis=-1, keepdims=True) + EPS) * gn_ref[:, cols]
            o_ref[rows, cols] = o * g_ref[rows, cols]
        return carry

    lax.fori_loop(0, nchunks, finish, 0)
    if want_state:
        if nprev:
            sfin_ref[0:nprev] = prev_ref[...]
        for direction in (0, 1):
            for h in range(hb):
                sfin_ref[nprev, direction, h] = st_ref[direction, h].T


def _gla_call(proj, gnorm, s0, prev, row0, nseq, n, hb, want_state, name):
    hd = LANES
    heads = A_HEADS
    hblocks = heads // hb
    blk0 = row0 // n

    def col(kind):
        return pl.BlockSpec((n, hb * hd), lambda b, h: (blk0 + b, kind * hblocks + h))

    in_specs = [col(0), col(1), col(2), col(3), col(4),
                pl.BlockSpec((1, hb * hd), lambda b, h: (0, h))]
    args = [proj, proj, proj, proj, proj, gnorm]
    state_spec = pl.BlockSpec((None, 2, hb, hd, hd), lambda b, h: (b, 0, h, 0, 0))
    if s0 is not None:
        in_specs.append(state_spec)
        args.append(s0)
    nprev = 0 if prev is None else prev.shape[1]

    def layered(layers):
        return pl.BlockSpec((None, layers, 2, hb, hd, hd), lambda b, h: (b, 0, 0, h, 0, 0))

    if nprev:
        in_specs.append(layered(nprev))
        args.append(prev)
    out_shape = [jax.ShapeDtypeStruct((nseq * n, heads * hd), F32)]
    out_specs = [pl.BlockSpec((n, hb * hd), lambda b, h: (b, h))]
    if want_state:
        out_shape.append(jax.ShapeDtypeStruct((nseq, nprev + 1, 2, heads, hd, hd), F32))
        out_specs.append(layered(nprev + 1))
    return pl.pallas_call(
        functools.partial(_gla_kernel, n=n, hb=hb, has_s0=s0 is not None, nprev=nprev, want_state=want_state),
        grid=(nseq, hblocks),
        in_specs=in_specs,
        out_specs=out_specs,
        out_shape=out_shape,
        scratch_shapes=[pltpu.VMEM((n, hb * hd), F32), pltpu.VMEM((n, hb * hd), F32),
                        pltpu.VMEM((2, hb, hd, hd), F32)],
        compiler_params=_params("parallel", "parallel"),
        name=name,
    )(*args)


def _stack_groups(ref):
    return jnp.concatenate([ref[:, g * LANES:(g + 1) * LANES] for g in range(GROUP)], axis=0)


def _sink_column(sink_ref, kh, rows):
    gi = _div(lax.broadcasted_iota(jnp.int32, (GROUP * rows, 1), 0), rows)
    col = jnp.zeros((GROUP * rows, 1), F32)
    for g in range(GROUP):
        col = jnp.where(gi == g, sink_ref[kh, g], col)
    return col


def _attn_ctx_kernel(sink_ref, q_ref, k_ref, v_ref, o_ref, *, scale):
    kh = pl.program_id(1)
    rows = q_ref.shape[0]
    q = (_stack_groups(q_ref) * scale).astype(BF16)
    lg = _dot_nt(q, k_ref[...].astype(BF16))
    sink = _sink_column(sink_ref, kh, rows)
    mx = jnp.maximum(jnp.max(lg, axis=-1, keepdims=True), sink)
    p = jnp.exp(lg - mx)
    den = jnp.sum(p, axis=-1, keepdims=True) + jnp.exp(sink - mx)
    o = _dot(p.astype(BF16), v_ref[...].astype(BF16)) * (1.0 / den)
    for g in range(GROUP):
        o_ref[:, g * LANES:(g + 1) * LANES] = o[g * rows:(g + 1) * rows, :]


def _attn_ctx_call(qkv, sink, nseq, n):
    hd = LANES
    kcol = ATT_HEADS
    vcol = ATT_HEADS + KV_HEADS
    return pl.pallas_call(
        functools.partial(_attn_ctx_kernel, scale=hd ** -0.5),
        grid=(nseq, KV_HEADS),
        in_specs=[
            pl.BlockSpec(memory_space=pltpu.SMEM),
            pl.BlockSpec((n, GROUP * hd), lambda b, kh: (b, kh)),
            pl.BlockSpec((n, hd), lambda b, kh: (b, kcol + kh)),
            pl.BlockSpec((n, hd), lambda b, kh: (b, vcol + kh)),
        ],
        out_specs=pl.BlockSpec((n, GROUP * hd), lambda b, kh: (b, kh)),
        out_shape=jax.ShapeDtypeStruct((nseq * n, ATT_HEADS * hd), F32),
        compiler_params=_params("parallel", "parallel"),
        name="attn_context",
    )(sink, qkv, qkv, qkv)


def _rope_kernel(x_ref, cos_ref, sa_ref, sb_ref, o_ref):
    cos = cos_ref[...]
    sa = sa_ref[...]
    sb = sb_ref[...]
    for h in range(x_ref.shape[1] // LANES):
        x = x_ref[:, h * LANES:(h + 1) * LANES]
        o_ref[:, h * LANES:(h + 1) * LANES] = (
            x * cos + pltpu.roll(x, LANES - 32, 1) * sa + pltpu.roll(x, 32, 1) * sb)


def _rope_call(qkv, tables, row0, nrows, n, tm):
    width = (ATT_HEADS + KV_HEADS) * LANES
    tab = pl.BlockSpec((tm, LANES), lambda i: (i % (n // tm), 0))
    return pl.pallas_call(
        _rope_kernel,
        grid=(nrows // tm,),
        in_specs=[pl.BlockSpec((tm, width), lambda i: (row0 // tm + i, 0)), tab, tab, tab],
        out_specs=pl.BlockSpec((tm, width), lambda i: (i, 0)),
        out_shape=jax.ShapeDtypeStruct((nrows, width), F32),
        compiler_params=_params("parallel"),
        name="axial_rope",
    )(qkv, *tables)


def _attn_lat_kernel(sink_ref, bias_ref, q_ref, k0_ref, k1_ref, k2_ref, v0_ref, v1_ref, v2_ref, kc_ref, vc_ref,
                     o_ref, *, scale, nblk):
    nb = pl.program_id(1)
    kh = pl.program_id(2)
    rows = q_ref.shape[0]
    q = (_stack_groups(q_ref) * scale).astype(BF16)
    keys = jnp.concatenate([kc_ref[...], k0_ref[...], k1_ref[...], k2_ref[...]], axis=0).astype(BF16)
    vals = jnp.concatenate([vc_ref[...], v0_ref[...], v1_ref[...], v2_ref[...]], axis=0).astype(BF16)
    lg = _dot_nt(q, keys)
    past = kc_ref.shape[0]
    before = lg[:, past:past + BLOCK] + (bias_ref[:, 0:BLOCK] + jnp.where(nb == 0, NEG_BIG, 0.0))
    after = lg[:, past + 2 * BLOCK:] + (bias_ref[:, BLOCK:] + jnp.where(nb == nblk - 1, NEG_BIG, 0.0))
    lg = jnp.concatenate([lg[:, 0:past], before, lg[:, past + BLOCK:past + 2 * BLOCK], after], axis=1)
    sink = _sink_column(sink_ref, kh, rows)
    mx = jnp.maximum(jnp.max(lg, axis=-1, keepdims=True), sink)
    p = jnp.exp(lg - mx)
    den = jnp.sum(p, axis=-1, keepdims=True) + jnp.exp(sink - mx)
    o = _dot(p.astype(BF16), vals) * (1.0 / den)
    for g in range(GROUP):
        o_ref[:, g * LANES:(g + 1) * LANES] = o[g * rows:(g + 1) * rows, :]


def _attn_lat_call(qk_rope, qkv, cache_k, cache_v, sink, row0, nseq, n):
    hd = LANES
    nblk = n // BLOCK
    past = cache_k.shape[1]
    vcol = ATT_HEADS + KV_HEADS
    vrow0 = row0 // BLOCK

    def near(off, base, colbase):
        def index(b, i, kh):
            return (base + b * nblk + jnp.clip(i + off, 0, nblk - 1), colbase + kh)
        return pl.BlockSpec((BLOCK, hd), index)

    t_in = jnp.arange(GROUP * BLOCK)[:, None] % BLOCK
    j_in = jnp.arange(BLOCK)[None, :]
    bias = jnp.concatenate([jnp.where(j_in >= t_in, 0.0, NEG_BIG), jnp.where(j_in <= t_in, 0.0, NEG_BIG)],
                           axis=1).astype(F32)
    cache_spec = pl.BlockSpec((None, past, hd), lambda b, i, kh: (b, 0, kh))
    return pl.pallas_call(
        functools.partial(_attn_lat_kernel, scale=hd ** -0.5, nblk=nblk),
        grid=(nseq, nblk, KV_HEADS),
        in_specs=[
            pl.BlockSpec(memory_space=pltpu.SMEM),
            pl.BlockSpec(bias.shape, lambda b, i, kh: (0, 0)),
            pl.BlockSpec((BLOCK, GROUP * hd), lambda b, i, kh: (b * nblk + i, kh)),
            near(-1, 0, ATT_HEADS), near(0, 0, ATT_HEADS), near(1, 0, ATT_HEADS),
            near(-1, vrow0, vcol), near(0, vrow0, vcol), near(1, vrow0, vcol),
            cache_spec, cache_spec,
        ],
        out_specs=pl.BlockSpec((BLOCK, GROUP * hd), lambda b, i, kh: (b * nblk + i, kh)),
        out_shape=jax.ShapeDtypeStruct((nseq * n, ATT_HEADS * hd), F32),
        compiler_params=_params("parallel", "parallel", "parallel"),
        name="attn_latent",
    )(sink, bias, qk_rope, qk_rope, qk_rope, qk_rope, qkv, qkv, qkv, cache_k, cache_v)


def _rope_tables(n):
    quarter = LANES // 4
    inv = ROPE_BASE ** (-jnp.arange(quarter, dtype=F32) / quarter)
    pos = jnp.arange(n)
    row_pos = (pos // GRID_W).astype(F32)
    col_pos = (pos % GRID_W).astype(F32)
    ang_r = row_pos[:, None] * inv[None, :]
    ang_c = col_pos[:, None] * inv[None, :]
    zero = jnp.zeros_like(ang_r)
    cos = jnp.concatenate([jnp.cos(ang_r), jnp.cos(ang_r), jnp.cos(ang_c), jnp.cos(ang_c)], axis=1)
    sin_up = jnp.concatenate([-jnp.sin(ang_r), zero, -jnp.sin(ang_c), zero], axis=1)
    sin_dn = jnp.concatenate([zero, jnp.sin(ang_r), zero, jnp.sin(ang_c)], axis=1)
    return cos, sin_up, sin_dn


def _lower_bounds(lb_param):
    p = jax.nn.softmax(lb_param.astype(F32), axis=0)
    cs = jnp.cumsum(p, axis=0)
    return cs - cs[:1]


def kernel(x_prompt, x_sample, cache_k, cache_v, state_hgrn, c, c_ctx, w_ada, b_ada, norm1, norm2, norm_final,
           w_gate_up, w_down, w_in_a, lower_bounds, gnorm_a, w_out_a, w_qkv_b, w_out_b, sink_b):
    batch, seq, d = x_prompt.shape
    dec_batch, dec_seq, _ = x_sample.shape
    depth = w_ada.shape[0]
    t_ctx = batch * seq
    t_lat = dec_batch * dec_seq
    assert dec_batch + 1 <= SUBLANES and t_ctx % TOKEN_TILE == 0 and dec_seq % TOKEN_TILE == 0
    assert WINDOW == BLOCK

    def cond_row(tok0):
        return jnp.where(tok0 < t_ctx, 0, 1 + (tok0 - t_ctx) // dec_seq)

    x = (x_prompt.reshape(t_ctx, d), x_sample.reshape(t_lat, d))
    cond8 = jnp.concatenate([c_ctx[None, :], c, jnp.zeros((SUBLANES - 1 - dec_batch, d), F32)], axis=0)
    mod = _ada_call(cond8, w_ada, b_ada).reshape(depth, SUBLANES, 1, 6 * d)
    lb_all = _lower_bounds(lower_bounds)
    rope_tabs = _rope_tables(dec_seq)
    past = cache_k.shape[2]
    nf = norm_final.reshape(1, d)

    new_k, new_v, new_s = [], [], None
    for l in range(depth):
        j = l // 2
        n1 = norm1[l].reshape(1, d)
        n2 = norm2[l].reshape(1, d)
        if l % 2 == 0:
            proj = _proj_call(x, mod[l], n1, w_in_a[j].astype(BF16), cond_row, HGRN_PROJ_TILE,
                              lb_all[j].reshape(1, 2 * d))
            gn = gnorm_a[j].reshape(1, d)
            a_ctx, new_s = _gla_call(proj, gn, None, new_s, 0, batch, seq, GLA_HEADS_CTX, True, "hgrn_scan_ctx")
            (a_lat,) = _gla_call(proj, gn, state_hgrn[:, j], None, t_ctx, dec_batch, dec_seq, GLA_HEADS_LAT,
                                 False, "hgrn_scan_lat")
            w_mix = w_out_a[j]
        else:
            qkv = _proj_call(x, mod[l], n1, w_qkv_b[j].astype(BF16), cond_row, TOKEN_TILE)
            sink = sink_b[j].reshape(KV_HEADS, GROUP)
            kv = qkv[:t_ctx, ATT_HEADS * LANES:].reshape(batch, seq, 2, KV_HEADS, LANES)
            new_k.append(kv[:, :, 0])
            new_v.append(kv[:, :, 1])
            a_ctx = _attn_ctx_call(qkv, sink, batch, seq)
            qk_rope = _rope_call(qkv, rope_tabs, t_ctx, t_lat, dec_seq, 512)
            a_lat = _attn_lat_call(qk_rope, qkv, cache_k[:, j].reshape(dec_batch, past, KV_HEADS * LANES),
                                   cache_v[:, j].reshape(dec_batch, past, KV_HEADS * LANES), sink,
                                   t_ctx, dec_batch, dec_seq)
            w_mix = w_out_b[j]
        x = _ffn_call(x, (a_ctx, a_lat), mod[l], n2, w_mix.astype(BF16), w_gate_up[l].astype(BF16),
                      w_down[l].astype(BF16), nf, cond_row, TOKEN_TILE, l == depth - 1)

    y_prompt = x[0].reshape(batch, seq, d)
    y_sample = x[1].reshape(dec_batch, dec_seq, d)
    return (y_prompt, y_sample, jnp.stack(new_k, axis=1), jnp.stack(new_v, axis=1), new_s)
```

```python
import functools

import jax
import jax.numpy as jnp
from jax import lax
from jax.experimental import pallas as pl
from jax.experimental.pallas import tpu as pltpu

F32 = jnp.float32
BF16 = jnp.bfloat16
EPS = 1e-6
ROPE_BASE = 10000.0

V7X_VMEM_LIMIT_BYTES = 56 * 1024 * 1024
SUBLANES = 8
LANES = 128

A_HEADS = 8
ATT_HEADS = 8
KV_HEADS = 2
GROUP = ATT_HEADS // KV_HEADS
WINDOW = 128
BLOCK = 128
GRID_W = 64
TOKEN_TILE = 512
HGRN_PROJ_TILE = 256
GLA_CHUNK = 128
GLA_LEAF = SUBLANES
GLA_HEADS_CTX = 8
GLA_HEADS_LAT = 2
GLA_BLOCK = 32
GLA_RANGE_BOUND = 86.0
LOG2_E = 1.4426950408889634
NEG_BIG = -1e30


def _params(*sem):
    return pltpu.CompilerParams(dimension_semantics=sem, vmem_limit_bytes=V7X_VMEM_LIMIT_BYTES)


def _resident(shape):
    nd = len(shape)
    return pl.BlockSpec(shape, lambda *_: (0,) * nd, pipeline_mode=pl.Buffered(1))


def _layer(stacked, layer):
    return pl.BlockSpec((None,) + stacked.shape[1:], lambda *_: (layer, 0, 0), pipeline_mode=pl.Buffered(1))


def _div(x, k):
    assert k & (k - 1) == 0
    return jnp.right_shift(x, k.bit_length() - 1)


def _mod(x, k):
    assert k & (k - 1) == 0
    return jnp.bitwise_and(x, k - 1)


def _sigmoid(x):
    return 1.0 / (1.0 + jnp.exp(-x))


def _silu(x):
    return x * _sigmoid(x)


def _rmsnorm(x, gain):
    return x * lax.rsqrt(jnp.mean(x * x, axis=-1, keepdims=True) + EPS) * gain


def _dot(a, b):
    return jnp.dot(a, b, preferred_element_type=F32)


def _dot_nt(a, b):
    return lax.dot_general(a, b, (((1,), (1,)), ((), ())), preferred_element_type=F32)


def _ada_kernel(cond_ref, w_ref, b_ref, out_ref):
    s = _silu(cond_ref[...]).astype(BF16)
    out_ref[...] = _dot(s, w_ref[...].astype(BF16)) + b_ref[...]


def _ada_call(cond8, w_ada, b_ada):
    depth, d, n = w_ada.shape
    tn = n // 4
    return pl.pallas_call(
        _ada_kernel,
        grid=(depth, n // tn),
        in_specs=[
            pl.BlockSpec((SUBLANES, d), lambda l, j: (0, 0)),
            pl.BlockSpec((None, d, tn), lambda l, j: (l, 0, j)),
            pl.BlockSpec((None, 1, tn), lambda l, j: (l, 0, j)),
        ],
        out_specs=pl.BlockSpec((None, SUBLANES, tn), lambda l, j: (l, 0, j)),
        out_shape=jax.ShapeDtypeStruct((depth, SUBLANES, n), F32),
        compiler_params=_params("parallel", "parallel"),
        name="adaln_mod",
    )(cond8, w_ada, b_ada.reshape(depth, 1, n))


def _split_specs(tm, width, tiles0):
    return [pl.BlockSpec((tm, width), lambda i: (jnp.minimum(i, tiles0 - 1), 0)),
            pl.BlockSpec((tm, width), lambda i: (jnp.maximum(i - tiles0, 0), 0))]


def _load_split(ref0, ref1, tiles0):
    return jnp.where(pl.program_id(0) < tiles0, ref0[...], ref1[...])


def _store_split(ref0, ref1, tiles0, value):
    @pl.when(pl.program_id(0) < tiles0)
    def _():
        ref0[...] = value

    @pl.when(pl.program_id(0) >= tiles0)
    def _():
        ref1[...] = value


def _modulated(x0_ref, x1_ref, mod_ref, gain_ref, tiles0, d):
    shift = mod_ref[:, 0:d]
    scale = mod_ref[:, d:2 * d]
    x = _load_split(x0_ref, x1_ref, tiles0)
    return (_rmsnorm(x, gain_ref[...]) * (1.0 + scale) + shift).astype(BF16)


def _proj_kernel(x0_ref, x1_ref, mod_ref, gain_ref, w_ref, out_ref, *, d, tiles0):
    out_ref[...] = _dot(_modulated(x0_ref, x1_ref, mod_ref, gain_ref, tiles0, d), w_ref[...])


def _hgrn_proj_kernel(x0_ref, x1_ref, mod_ref, gain_ref, lb_ref, w_ref, out_ref, *, d, tiles0):
    h = _modulated(x0_ref, x1_ref, mod_ref, gain_ref, tiles0, d)
    for part in range(5):
        cols = slice(part * d, (part + 1) * d)
        p = _dot(h, w_ref[:, cols])
        if part in (0, 4):
            p = _silu(p)
        elif part in (1, 2):
            lb = lb_ref[:, (part - 1) * d:part * d]
            p = jnp.log2(lb + (1.0 - lb) * _sigmoid(p))
        out_ref[:, cols] = p


def _proj_call(x, mod, gain, w, layer, cond_row, tm, lb=None):
    x0, x1 = x
    d = x0.shape[1]
    t = x0.shape[0] + x1.shape[0]
    n = w.shape[2]
    tiles0 = x0.shape[0] // tm
    hgrn = lb is not None
    extra_specs = [_resident((1, 2 * d))] if hgrn else []
    extra_args = [lb] if hgrn else []
    return pl.pallas_call(
        functools.partial(_hgrn_proj_kernel if hgrn else _proj_kernel, d=d, tiles0=tiles0),
        grid=(t // tm,),
        in_specs=_split_specs(tm, d, tiles0) + [
            pl.BlockSpec((None, 1, mod.shape[-1]), lambda i: (cond_row(i * tm), 0, 0)),
            _resident((1, d)),
        ] + extra_specs + [_layer(w, layer)],
        out_specs=pl.BlockSpec((tm, n), lambda i: (i, 0)),
        out_shape=jax.ShapeDtypeStruct((t, n), F32),
        compiler_params=_params("parallel"),
        name="hgrn_in_proj" if hgrn else "attn_qkv_proj",
    )(x0, x1, mod, gain, *extra_args, w)


def _ffn_kernel(x0_ref, x1_ref, a0_ref, a1_ref, mod_ref, gain_ref, wo_ref, wgu_ref, wd_ref, nf_ref, *out_refs,
                d, d_ff, ff_chunk, final, tiles0):
    g1 = mod_ref[:, 2 * d:3 * d]
    sh2 = mod_ref[:, 3 * d:4 * d]
    sc2 = mod_ref[:, 4 * d:5 * d]
    g2 = mod_ref[:, 5 * d:6 * d]
    a = _load_split(a0_ref, a1_ref, tiles0)
    x1 = _load_split(x0_ref, x1_ref, tiles0) + g1 * _dot(a.astype(BF16), wo_ref[...])
    h = (_rmsnorm(x1, gain_ref[...]) * (1.0 + sc2) + sh2).astype(BF16)
    y = None
    for j in range(d_ff // ff_chunk):
        lo = j * ff_chunk
        g = _dot(h, wgu_ref[:, lo:lo + ff_chunk])
        u = _dot(h, wgu_ref[:, d_ff + lo:d_ff + lo + ff_chunk])
        part = _dot((_silu(g) * u).astype(BF16), wd_ref[lo:lo + ff_chunk, :])
        y = part if y is None else y + part
    x2 = x1 + g2 * y
    if final:
        x2 = _rmsnorm(x2, nf_ref[...])
    _store_split(out_refs[0], out_refs[1], tiles0, x2)


def _ffn_call(x, a, mod, gain, w_out, mixer_layer, w_gu, w_d, layer, norm_final, cond_row, tm, final):
    (x0, x1), (a0, a1) = x, a
    d = x0.shape[1]
    d_ff = w_d.shape[1]
    ff_chunk = d_ff
    tiles0 = x0.shape[0] // tm
    rows = _split_specs(tm, d, tiles0)
    return pl.pallas_call(
        functools.partial(_ffn_kernel, d=d, d_ff=d_ff, ff_chunk=ff_chunk, final=final, tiles0=tiles0),
        grid=((x0.shape[0] + x1.shape[0]) // tm,),
        in_specs=rows + rows + [
            pl.BlockSpec((None, 1, mod.shape[-1]), lambda i: (cond_row(i * tm), 0, 0)),
            _resident((1, d)),
            _layer(w_out, mixer_layer),
            _layer(w_gu, layer),
            _layer(w_d, layer),
            _resident((1, d)),
        ],
        out_specs=rows,
        out_shape=[jax.ShapeDtypeStruct(x0.shape, F32), jax.ShapeDtypeStruct(x1.shape, F32)],
        compiler_params=_params("arbitrary"),
        name="outproj_ffn",
    )(x0, x1, a0, a1, mod, gain, w_out, w_gu, w_d, norm_final)


def _row_blocks(b, rows, size):
    parts = [jnp.broadcast_to(b[r:r + 1, :], (size, b.shape[1])) for r in rows]
    return parts[0] if len(parts) == 1 else jnp.concatenate(parts, axis=0)


def _scores_blockwise(q, k, b, reverse):
    c, dk = q.shape
    blk = GLA_BLOCK
    nblk = c // blk
    zero_row = jnp.zeros((1, dk), F32)
    rows = []
    for i in range(nblk):
        lo, hi = i * blk, (i + 1) * blk
        if reverse:
            r = b[hi:hi + 1, :] if i < nblk - 1 else zero_row
            key_lo, key_hi = lo, c
        else:
            r = b[lo - 1:lo, :] if i > 0 else zero_row
            key_lo, key_hi = 0, hi
        qi = (q[lo:hi] * jnp.exp2(b[lo:hi] - r)).astype(BF16)
        ki = (k[key_lo:key_hi] * jnp.exp2(r - b[key_lo:key_hi])).astype(BF16)
        if key_hi - key_lo < c:
            pad = jnp.zeros((c - (key_hi - key_lo), dk), BF16)
            ki = jnp.concatenate([pad, ki] if reverse else [ki, pad], axis=0)
        rows.append(_dot_nt(qi, ki))
    return jnp.concatenate(rows, axis=0)


def _causal(c, reverse):
    t_idx = lax.broadcasted_iota(jnp.int32, (c, c), 0)
    s_idx = lax.broadcasted_iota(jnp.int32, (c, c), 1)
    return (s_idx >= t_idx) if reverse else (s_idx <= t_idx)


def _gla_decay_sums(lg, tri):
    hi = lg.astype(BF16)
    mid = (lg - hi.astype(F32)).astype(BF16)
    return _dot(jnp.where(tri, 1.0, 0.0).astype(BF16), jnp.concatenate([hi, mid], axis=1))


def _gla_mix(q, lg, b2, v, st, tri, reverse, bounded):
    c = q.shape[0]
    dk = q.shape[1]
    k = 1.0 - jnp.exp2(lg)
    b = b2[:, 0:dk] + b2[:, dk:2 * dk]
    b_end = b[0:1, :] if reverse else b[c - 1:c, :]
    o = _dot_nt((q * jnp.exp2(b)).astype(BF16), st.astype(BF16))
    k_dec = (k * jnp.exp2(b_end - b)).astype(BF16)
    st_new = st * jnp.exp2(b_end) + _dot(v.T.astype(BF16), k_dec)
    if bounded:
        scores = jnp.where(tri, _scores_blockwise(q, k, b, reverse), 0.0)
    else:
        scores = _scores_robust(q, k, b, tri, reverse)
    return o, st_new, scores.astype(BF16)


def _scores_robust(q, k, b, tri, reverse):
    c, dk = q.shape
    t_idx = lax.broadcasted_iota(jnp.int32, (c, c), 0)
    s_idx = lax.broadcasted_iota(jnp.int32, (c, c), 1)
    pos = lax.broadcasted_iota(jnp.int32, (c, dk), 0)
    scores = jnp.zeros((c, c), F32)
    m = c // 2
    while m >= GLA_LEAF:
        nblk = c // (2 * m)
        bound = [kb * 2 * m + (m if reverse else m - 1) for kb in range(nblk)]
        e = jnp.exp2(-jnp.abs(b - _row_blocks(b, bound, 2 * m)))
        late = _mod(pos, 2 * m) >= m
        q_side = jnp.logical_not(late) if reverse else late
        ql = jnp.where(q_side, q * e, 0.0).astype(BF16)
        kl = jnp.where(q_side, 0.0, k * e).astype(BF16)
        s_l = _dot_nt(ql, kl)
        if nblk > 1:
            s_l = jnp.where(_div(t_idx, 2 * m) == _div(s_idx, 2 * m), s_l, 0.0)
        scores = scores + s_l
        m //= 2
    nleaf = c // GLA_LEAF
    k3 = k.reshape(nleaf, GLA_LEAF, dk)
    b3 = b.reshape(nleaf, GLA_LEAF, dk)
    sel_d = _mod(lax.broadcasted_iota(jnp.int32, (dk, c), 1), GLA_LEAF)
    r = jnp.zeros((c, c), F32)
    for sl in range(GLA_LEAF):
        ks = jnp.broadcast_to(k3[:, sl:sl + 1, :], k3.shape).reshape(c, dk)
        bs = jnp.broadcast_to(b3[:, sl:sl + 1, :], b3.shape).reshape(c, dk)
        term = (q * ks * jnp.exp2(jnp.minimum(b - bs, 0.0))).astype(BF16)
        r = r + _dot(term, jnp.where(sel_d == sl, 1.0, 0.0).astype(BF16))
    same_leaf = _div(t_idx, GLA_LEAF) == _div(s_idx, GLA_LEAF)
    return scores + jnp.where(jnp.logical_and(same_leaf, tri), r, 0.0)


def _gla_kernel(*refs, n, hb, has_s0, nprev, want_state):
    q_ref, lgf_ref, lgb_ref, v_ref, g_ref, gn_ref = refs[:6]
    pos = 6
    s0_ref = prev_ref = sfin_ref = None
    if has_s0:
        s0_ref = refs[pos]
        pos += 1
    if nprev:
        prev_ref = refs[pos]
        pos += 1
    o_ref = refs[pos]
    pos += 1
    if want_state:
        sfin_ref = refs[pos]
        pos += 1
    of_ref, ob_ref, st_ref = refs[pos:pos + 3]
    c = GLA_CHUNK
    hd = LANES
    nchunks = n // c

    for direction in (0, 1):
        for h in range(hb):
            if has_s0:
                st_ref[direction, h] = s0_ref[direction, h].T
            else:
                st_ref[direction, h] = jnp.zeros((hd, hd), F32)

    def block_decay(i, low):
        rows = pl.ds(pl.multiple_of(i * c, c), c)
        for lg_ref in (lgf_ref, lgb_ref):
            for h in range(hb):
                lg = lg_ref[rows, h * hd:(h + 1) * hd].reshape(c // GLA_BLOCK, GLA_BLOCK, hd)
                low = jnp.minimum(low, jnp.min(jnp.sum(lg, axis=1), axis=0, keepdims=True))
        return low

    low = lax.fori_loop(0, nchunks, block_decay, jnp.zeros((1, hd), F32))
    bounded = jnp.min(low) >= -GLA_RANGE_BOUND

    def scan(is_bounded):
        def body(i, carry):
            tri = (_causal(c, False), _causal(c, True))
            chains = []
            for h in range(hb):
                cols = slice(h * hd, (h + 1) * hd)
                for direction in (0, 1):
                    ci = i if direction == 0 else nchunks - 1 - i
                    rows = pl.ds(pl.multiple_of(ci * c, c), c)
                    lg = (lgf_ref if direction == 0 else lgb_ref)[rows, cols]
                    chains.append((h, direction, rows, cols, lg, _gla_decay_sums(lg, tri[direction])))
            mixed = []
            for h, direction, rows, cols, lg, b2 in chains:
                mixed.append(_gla_mix(q_ref[rows, cols], lg, b2, v_ref[rows, cols], st_ref[direction, h],
                                      tri[direction], direction == 1, is_bounded))
            for (h, direction, rows, cols, _, _), (o, st_new, scores) in zip(chains, mixed):
                st_ref[direction, h] = st_new
                o = o + _dot(scores, v_ref[rows, cols].astype(BF16))
                (of_ref if direction == 0 else ob_ref)[rows, cols] = o
            return carry

        lax.fori_loop(0, nchunks, body, 0)

    pl.when(bounded)(functools.partial(scan, True))
    pl.when(jnp.logical_not(bounded))(functools.partial(scan, False))

    def finish(i, carry):
        rows = pl.ds(pl.multiple_of(i * c, c), c)
        for h in range(hb):
            cols = slice(h * hd, (h + 1) * hd)
            o = of_ref[rows, cols] + ob_ref[rows, cols]
            o = o * lax.rsqrt(jnp.mean(o * o, axis=-1, keepdims=True) + EPS) * gn_ref[:, cols]
            o_ref[rows, cols] = o * g_ref[rows, cols]
        return carry

    lax.fori_loop(0, nchunks, finish, 0)
    if want_state:
        if nprev:
            sfin_ref[0:nprev] = prev_ref[...]
        for direction in (0, 1):
            for h in range(hb):
                sfin_ref[nprev, direction, h] = st_ref[direction, h].T


def _gla_call(proj, gnorm, s0, prev, row0, nseq, n, hb, want_state, name):
    hd = LANES
    heads = A_HEADS
    hblocks = heads // hb
    blk0 = row0 // n

    def col(kind):
        return pl.BlockSpec((n, hb * hd), lambda b, h: (blk0 + b, kind * hblocks + h))

    in_specs = [col(0), col(1), col(2), col(3), col(4),
                pl.BlockSpec((1, hb * hd), lambda b, h: (0, h))]
    args = [proj, proj, proj, proj, proj, gnorm]
    state_spec = pl.BlockSpec((None, 2, hb, hd, hd), lambda b, h: (b, 0, h, 0, 0))
    if s0 is not None:
        in_specs.append(state_spec)
        args.append(s0)
    nprev = 0 if prev is None else prev.shape[1]

    def layered(layers):
        return pl.BlockSpec((None, layers, 2, hb, hd, hd), lambda b, h: (b, 0, 0, h, 0, 0))

    if nprev:
        in_specs.append(layered(nprev))
        args.append(prev)
    out_shape = [jax.ShapeDtypeStruct((nseq * n, heads * hd), F32)]
    out_specs = [pl.BlockSpec((n, hb * hd), lambda b, h: (b, h))]
    if want_state:
        out_shape.append(jax.ShapeDtypeStruct((nseq, nprev + 1, 2, heads, hd, hd), F32))
        out_specs.append(layered(nprev + 1))
    return pl.pallas_call(
        functools.partial(_gla_kernel, n=n, hb=hb, has_s0=s0 is not None, nprev=nprev, want_state=want_state),
        grid=(nseq, hblocks),
        in_specs=in_specs,
        out_specs=out_specs,
        out_shape=out_shape,
        scratch_shapes=[pltpu.VMEM((n, hb * hd), F32), pltpu.VMEM((n, hb * hd), F32),
                        pltpu.VMEM((2, hb, hd, hd), F32)],
        compiler_params=_params("parallel", "parallel"),
        name=name,
    )(*args)


def _stacked_queries(q_ref, kh, scale):
    heads = [q_ref[:, (kh * GROUP + g) * LANES:(kh * GROUP + g + 1) * LANES] for g in range(GROUP)]
    return (jnp.concatenate(heads, axis=0) * scale).astype(BF16)


def _sink_column(sink_ref, kh, rows):
    gi = _div(lax.broadcasted_iota(jnp.int32, (GROUP * rows, 1), 0), rows)
    col = jnp.zeros((GROUP * rows, 1), F32)
    for g in range(GROUP):
        col = jnp.where(gi == g, sink_ref[kh, g] * LOG2_E, col)
    return col


def _softmax_pv(logits, sinks, values):
    mx = [jnp.maximum(jnp.max(lg, axis=-1, keepdims=True), s) for lg, s in zip(logits, sinks)]
    p = [jnp.exp2(lg - m) for lg, m in zip(logits, mx)]
    den = [jnp.sum(pi, axis=-1, keepdims=True) + jnp.exp2(s - m) for pi, s, m in zip(p, sinks, mx)]
    return [_dot(pi.astype(BF16), v) * (1.0 / d) for pi, v, d in zip(p, values, den)]


def _store_heads(o_ref, kh, o):
    rows = o_ref.shape[0]
    for g in range(GROUP):
        o_ref[:, (kh * GROUP + g) * LANES:(kh * GROUP + g + 1) * LANES] = o[g * rows:(g + 1) * rows, :]


def _attn_ctx_kernel(sink_ref, q_ref, k_ref, v_ref, *rest, scale, nprev):
    if nprev:
        pk_ref, pv_ref, o_ref, nk_ref, nv_ref = rest
        nk_ref[0:nprev] = pk_ref[...]
        nv_ref[0:nprev] = pv_ref[...]
    else:
        o_ref, nk_ref, nv_ref = rest
    nk_ref[nprev] = k_ref[...]
    nv_ref[nprev] = v_ref[...]
    rows = q_ref.shape[0]
    heads = range(KV_HEADS)
    q = [_stacked_queries(q_ref, kh, scale) for kh in heads]
    logits = [_dot_nt(q[kh], k_ref[:, kh * LANES:(kh + 1) * LANES].astype(BF16)) for kh in heads]
    sinks = [_sink_column(sink_ref, kh, rows) for kh in heads]
    values = [v_ref[:, kh * LANES:(kh + 1) * LANES].astype(BF16) for kh in heads]
    for kh, o in zip(heads, _softmax_pv(logits, sinks, values)):
        _store_heads(o_ref, kh, o)


def _attn_ctx_call(qkv, sink, prev_k, prev_v, nseq, n):
    hd = LANES
    kvw = KV_HEADS * hd
    nprev = 0 if prev_k is None else prev_k.shape[1]

    def layered(layers):
        return pl.BlockSpec((None, layers, n, kvw), lambda b: (b, 0, 0, 0))

    kv_shape = jax.ShapeDtypeStruct((nseq, nprev + 1, n, kvw), F32)
    prev_specs = [layered(nprev), layered(nprev)] if nprev else []
    prev_args = [prev_k, prev_v] if nprev else []
    return pl.pallas_call(
        functools.partial(_attn_ctx_kernel, scale=hd ** -0.5 * LOG2_E, nprev=nprev),
        grid=(nseq,),
        in_specs=[
            pl.BlockSpec(memory_space=pltpu.SMEM),
            pl.BlockSpec((n, ATT_HEADS * hd), lambda b: (b, 0)),
            pl.BlockSpec((n, kvw), lambda b: (b, ATT_HEADS // KV_HEADS)),
            pl.BlockSpec((n, kvw), lambda b: (b, ATT_HEADS // KV_HEADS + 1)),
        ] + prev_specs,
        out_specs=[pl.BlockSpec((n, ATT_HEADS * hd), lambda b: (b, 0)), layered(nprev + 1), layered(nprev + 1)],
        out_shape=[jax.ShapeDtypeStruct((nseq * n, ATT_HEADS * hd), F32), kv_shape, kv_shape],
        compiler_params=_params("parallel"),
        name="attn_context",
    )(sink, qkv, qkv, qkv, *prev_args)


def _rope_kernel(x_ref, cos_ref, sa_ref, sb_ref, o_ref):
    cos = cos_ref[...]
    sa = sa_ref[...]
    sb = sb_ref[...]
    for h in range(x_ref.shape[1] // LANES):
        x = x_ref[:, h * LANES:(h + 1) * LANES]
        o_ref[:, h * LANES:(h + 1) * LANES] = (
            x * cos + pltpu.roll(x, LANES - 32, 1) * sa + pltpu.roll(x, 32, 1) * sb)


def _rope_call(qkv, tables, row0, nrows, n, tm):
    width = (ATT_HEADS + KV_HEADS) * LANES
    tab = pl.BlockSpec((tm, LANES), lambda i: (i % (n // tm), 0))
    return pl.pallas_call(
        _rope_kernel,
        grid=(nrows // tm,),
        in_specs=[pl.BlockSpec((tm, width), lambda i: (row0 // tm + i, 0)), tab, tab, tab],
        out_specs=pl.BlockSpec((tm, width), lambda i: (i, 0)),
        out_shape=jax.ShapeDtypeStruct((nrows, width), F32),
        compiler_params=_params("parallel"),
        name="axial_rope",
    )(qkv, *tables)


def _attn_lat_kernel(sink_ref, bias_ref, q_ref, k0_ref, k1_ref, k2_ref, v0_ref, v1_ref, v2_ref, kc_ref, vc_ref,
                     o_ref, *, scale, nblk):
    nb = pl.program_id(1)
    rows = q_ref.shape[0]
    past = kc_ref.shape[0]
    heads = range(KV_HEADS)
    bias_before = bias_ref[:, 0:BLOCK] + jnp.where(nb == 0, NEG_BIG, 0.0)
    bias_after = bias_ref[:, BLOCK:] + jnp.where(nb == nblk - 1, NEG_BIG, 0.0)

    def rows_of(kh, refs):
        return jnp.concatenate([r[:, kh * LANES:(kh + 1) * LANES] for r in refs], axis=0).astype(BF16)

    q = [_stacked_queries(q_ref, kh, scale) for kh in heads]
    raw = [_dot_nt(q[kh], rows_of(kh, (kc_ref, k0_ref, k1_ref, k2_ref))) for kh in heads]
    logits = [jnp.concatenate([lg[:, 0:past], lg[:, past:past + BLOCK] + bias_before,
                               lg[:, past + BLOCK:past + 2 * BLOCK], lg[:, past + 2 * BLOCK:] + bias_after], axis=1)
              for lg in raw]
    sinks = [_sink_column(sink_ref, kh, rows) for kh in heads]
    values = [rows_of(kh, (vc_ref, v0_ref, v1_ref, v2_ref)) for kh in heads]
    for kh, o in zip(heads, _softmax_pv(logits, sinks, values)):
        _store_heads(o_ref, kh, o)


def _attn_lat_call(qk_rope, qkv, cache_k, cache_v, sink, row0, nseq, n):
    hd = LANES
    nblk = n // BLOCK
    past = cache_k.shape[1]
    kcol = ATT_HEADS // KV_HEADS
    vrow0 = row0 // BLOCK

    def near(off, base, colblock):
        def index(b, i):
            return (base + b * nblk + jnp.clip(i + off, 0, nblk - 1), colblock)
        return pl.BlockSpec((BLOCK, KV_HEADS * hd), index)

    t_in = jnp.arange(GROUP * BLOCK)[:, None] % BLOCK
    j_in = jnp.arange(BLOCK)[None, :]
    bias = jnp.concatenate([jnp.where(j_in >= t_in, 0.0, NEG_BIG), jnp.where(j_in <= t_in, 0.0, NEG_BIG)],
                           axis=1).astype(F32)
    cache_spec = pl.BlockSpec((None, past, KV_HEADS * hd), lambda b, i: (b, 0, 0))
    return pl.pallas_call(
        functools.partial(_attn_lat_kernel, scale=hd ** -0.5 * LOG2_E, nblk=nblk),
        grid=(nseq, nblk),
        in_specs=[
            pl.BlockSpec(memory_space=pltpu.SMEM),
            pl.BlockSpec(bias.shape, lambda b, i: (0, 0)),
            pl.BlockSpec((BLOCK, ATT_HEADS * hd), lambda b, i: (b * nblk + i, 0)),
            near(-1, 0, kcol), near(0, 0, kcol), near(1, 0, kcol),
            near(-1, vrow0, kcol + 1), near(0, vrow0, kcol + 1), near(1, vrow0, kcol + 1),
            cache_spec, cache_spec,
        ],
        out_specs=pl.BlockSpec((BLOCK, ATT_HEADS * hd), lambda b, i: (b * nblk + i, 0)),
        out_shape=jax.ShapeDtypeStruct((nseq * n, ATT_HEADS * hd), F32),
        compiler_params=_params("parallel", "parallel"),
        name="attn_latent",
    )(sink, bias, qk_rope, qk_rope, qk_rope, qk_rope, qkv, qkv, qkv, cache_k, cache_v)


def _rope_tables(n):
    quarter = LANES // 4
    inv = ROPE_BASE ** (-jnp.arange(quarter, dtype=F32) / quarter)
    pos = jnp.arange(n)
    row_pos = (pos // GRID_W).astype(F32)
    col_pos = (pos % GRID_W).astype(F32)
    ang_r = row_pos[:, None] * inv[None, :]
    ang_c = col_pos[:, None] * inv[None, :]
    zero = jnp.zeros_like(ang_r)
    cos = jnp.concatenate([jnp.cos(ang_r), jnp.cos(ang_r), jnp.cos(ang_c), jnp.cos(ang_c)], axis=1)
    sin_up = jnp.concatenate([-jnp.sin(ang_r), zero, -jnp.sin(ang_c), zero], axis=1)
    sin_dn = jnp.concatenate([zero, jnp.sin(ang_r), zero, jnp.sin(ang_c)], axis=1)
    return cos, sin_up, sin_dn


def _lower_bounds(lb_param):
    p = jax.nn.softmax(lb_param.astype(F32), axis=0)
    cs = jnp.cumsum(p, axis=0)
    return cs - cs[:1]


def kernel(x_prompt, x_sample, cache_k, cache_v, state_hgrn, c, c_ctx, w_ada, b_ada, norm1, norm2, norm_final,
           w_gate_up, w_down, w_in_a, lower_bounds, gnorm_a, w_out_a, w_qkv_b, w_out_b, sink_b):
    batch, seq, d = x_prompt.shape
    dec_batch, dec_seq, _ = x_sample.shape
    depth = w_ada.shape[0]
    t_ctx = batch * seq
    t_lat = dec_batch * dec_seq
    assert dec_batch + 1 <= SUBLANES and t_ctx % TOKEN_TILE == 0 and dec_seq % TOKEN_TILE == 0
    assert WINDOW == BLOCK

    def cond_row(tok0):
        return jnp.where(tok0 < t_ctx, 0, 1 + (tok0 - t_ctx) // dec_seq)

    x = (x_prompt.reshape(t_ctx, d), x_sample.reshape(t_lat, d))
    cond8 = jnp.concatenate([c_ctx[None, :], c, jnp.zeros((SUBLANES - 1 - dec_batch, d), F32)], axis=0)
    mod = _ada_call(cond8, w_ada, b_ada).reshape(depth, SUBLANES, 1, 6 * d)
    lb_all = _lower_bounds(lower_bounds)
    rope_tabs = _rope_tables(dec_seq)
    past = cache_k.shape[2]
    nf = norm_final.reshape(1, d)

    w_in_a, w_qkv_b, w_out_a, w_out_b, w_gate_up, w_down = (
        w.astype(BF16) for w in (w_in_a, w_qkv_b, w_out_a, w_out_b, w_gate_up, w_down))
    new_k = new_v = new_s = None
    for l in range(depth):
        j = l // 2
        n1 = norm1[l].reshape(1, d)
        n2 = norm2[l].reshape(1, d)
        if l % 2 == 0:
            proj = _proj_call(x, mod[l], n1, w_in_a, j, cond_row, HGRN_PROJ_TILE, lb_all[j].reshape(1, 2 * d))
            gn = gnorm_a[j].reshape(1, d)
            a_ctx, new_s = _gla_call(proj, gn, None, new_s, 0, batch, seq, GLA_HEADS_CTX, True, "hgrn_scan_ctx")
            (a_lat,) = _gla_call(proj, gn, state_hgrn[:, j], None, t_ctx, dec_batch, dec_seq, GLA_HEADS_LAT,
                                 False, "hgrn_scan_lat")
            w_mix = w_out_a
        else:
            qkv = _proj_call(x, mod[l], n1, w_qkv_b, j, cond_row, TOKEN_TILE)
            sink = sink_b[j].reshape(KV_HEADS, GROUP)
            a_ctx, new_k, new_v = _attn_ctx_call(qkv, sink, new_k, new_v, batch, seq)
            qk_rope = _rope_call(qkv, rope_tabs, t_ctx, t_lat, dec_seq, 512)
            a_lat = _attn_lat_call(qk_rope, qkv, cache_k[:, j].reshape(dec_batch, past, KV_HEADS * LANES),
                                   cache_v[:, j].reshape(dec_batch, past, KV_HEADS * LANES), sink,
                                   t_ctx, dec_batch, dec_seq)
            w_mix = w_out_b
        x = _ffn_call(x, (a_ctx, a_lat), mod[l], n2, w_mix, j, w_gate_up, w_down, l, nf, cond_row, TOKEN_TILE,
                      l == depth - 1)

    y_prompt = x[0].reshape(batch, seq, d)
    y_sample = x[1].reshape(dec_batch, dec_seq, d)
    kv_shape = (batch, new_k.shape[1], seq, KV_HEADS, LANES)
    return (y_prompt, y_sample, new_k.reshape(kv_shape), new_v.reshape(kv_shape), new_s)
```

```python
import functools

import jax
import jax.numpy as jnp
from jax import lax
from jax.experimental import pallas as pl
from jax.experimental.pallas import tpu as pltpu

F32 = jnp.float32
BF16 = jnp.bfloat16
EPS = 1e-6
ROPE_BASE = 10000.0

V7X_VMEM_LIMIT_BYTES = 56 * 1024 * 1024
SUBLANES = 8
LANES = 128

A_HEADS = 8
ATT_HEADS = 8
KV_HEADS = 2
GROUP = ATT_HEADS // KV_HEADS
WINDOW = 128
BLOCK = 128
GRID_W = 64
TOKEN_TILE = 512
HGRN_PROJ_TILE = 256
GLA_CHUNK = 128
GLA_LEAF = SUBLANES
GLA_HEADS_CTX = 8
GLA_HEADS_LAT = 2
GLA_BLOCK = 32
GLA_RANGE_BOUND = 86.0
LOG2_E = 1.4426950408889634
NEG_BIG = -1e30


def _params(*sem):
    return pltpu.CompilerParams(dimension_semantics=sem, vmem_limit_bytes=V7X_VMEM_LIMIT_BYTES)


def _resident(shape):
    nd = len(shape)
    return pl.BlockSpec(shape, lambda *_: (0,) * nd, pipeline_mode=pl.Buffered(1))


def _layer(stacked, layer):
    return pl.BlockSpec((None,) + stacked.shape[1:], lambda *_: (layer, 0, 0), pipeline_mode=pl.Buffered(1))


def _div(x, k):
    assert k & (k - 1) == 0
    return jnp.right_shift(x, k.bit_length() - 1)


def _mod(x, k):
    assert k & (k - 1) == 0
    return jnp.bitwise_and(x, k - 1)


def _sigmoid(x):
    return 1.0 / (1.0 + jnp.exp(-x))


def _silu(x):
    return x * _sigmoid(x)


def _rmsnorm(x, gain):
    return x * lax.rsqrt(jnp.mean(x * x, axis=-1, keepdims=True) + EPS) * gain


def _dot(a, b):
    return jnp.dot(a, b, preferred_element_type=F32)


def _dot_nt(a, b):
    return lax.dot_general(a, b, (((1,), (1,)), ((), ())), preferred_element_type=F32)


def _ada_kernel(cond_ref, w_ref, b_ref, out_ref):
    s = _silu(cond_ref[...]).astype(BF16)
    out_ref[...] = _dot(s, w_ref[...].astype(BF16)) + b_ref[...]


def _ada_call(cond8, w_ada, b_ada):
    depth, d, n = w_ada.shape
    tn = n // 4
    return pl.pallas_call(
        _ada_kernel,
        grid=(depth, n // tn),
        in_specs=[
            pl.BlockSpec((SUBLANES, d), lambda l, j: (0, 0)),
            pl.BlockSpec((None, d, tn), lambda l, j: (l, 0, j)),
            pl.BlockSpec((None, 1, tn), lambda l, j: (l, 0, j)),
        ],
        out_specs=pl.BlockSpec((None, SUBLANES, tn), lambda l, j: (l, 0, j)),
        out_shape=jax.ShapeDtypeStruct((depth, SUBLANES, n), F32),
        compiler_params=_params("parallel", "parallel"),
        name="adaln_mod",
    )(cond8, w_ada, b_ada.reshape(depth, 1, n))


def _split_specs(tm, width, tiles0):
    return [pl.BlockSpec((tm, width), lambda i: (jnp.minimum(i, tiles0 - 1), 0)),
            pl.BlockSpec((tm, width), lambda i: (jnp.maximum(i - tiles0, 0), 0))]


def _load_split(ref0, ref1, tiles0):
    return jnp.where(pl.program_id(0) < tiles0, ref0[...], ref1[...])


def _store_split(ref0, ref1, tiles0, value):
    @pl.when(pl.program_id(0) < tiles0)
    def _():
        ref0[...] = value

    @pl.when(pl.program_id(0) >= tiles0)
    def _():
        ref1[...] = value


def _modulated(x0_ref, x1_ref, mod_ref, gain_ref, tiles0, d):
    shift = mod_ref[:, 0:d]
    scale = mod_ref[:, d:2 * d]
    x = _load_split(x0_ref, x1_ref, tiles0)
    return (_rmsnorm(x, gain_ref[...]) * (1.0 + scale) + shift).astype(BF16)


def _qkv_proj_kernel(x0_ref, x1_ref, mod_ref, gain_ref, w_ref, cos_ref, sa_ref, sb_ref, out_ref, *, d, tiles0):
    p = _dot(_modulated(x0_ref, x1_ref, mod_ref, gain_ref, tiles0, d), w_ref[...])
    cos = cos_ref[...]
    sa = sa_ref[...]
    sb = sb_ref[...]
    for h in range(p.shape[1] // LANES):
        x = p[:, h * LANES:(h + 1) * LANES]
        if h < ATT_HEADS + KV_HEADS:
            x = x * cos + pltpu.roll(x, LANES - 32, 1) * sa + pltpu.roll(x, 32, 1) * sb
        out_ref[:, h * LANES:(h + 1) * LANES] = x


def _hgrn_proj_kernel(x0_ref, x1_ref, mod_ref, gain_ref, lb_ref, w_ref, out_ref, low_ref, *, d, tiles0):
    h = _modulated(x0_ref, x1_ref, mod_ref, gain_ref, tiles0, d)
    low = None
    for part in range(5):
        cols = slice(part * d, (part + 1) * d)
        p = _dot(h, w_ref[:, cols])
        if part in (0, 4):
            p = _silu(p)
        elif part in (1, 2):
            lb = lb_ref[:, (part - 1) * d:part * d]
            p = jnp.log2(lb + (1.0 - lb) * _sigmoid(p))
            blocks = jnp.sum(p.reshape(p.shape[0] // GLA_BLOCK, GLA_BLOCK, d), axis=1)
            part_low = jnp.min(blocks, axis=0, keepdims=True)
            low = part_low if low is None else jnp.minimum(low, part_low)
        out_ref[:, cols] = p
    low_ref[...] = low


def _proj_call(x, mod, gain, w, layer, cond_row, tm, lb=None, rope=None):
    x0, x1 = x
    d = x0.shape[1]
    t = x0.shape[0] + x1.shape[0]
    tiles0 = x0.shape[0] // tm
    if lb is not None:
        body, name = _hgrn_proj_kernel, "hgrn_in_proj"
        extra_specs, extra_args = [_resident((1, 2 * d)), _layer(w, layer)], [lb, w]
    else:
        tables, n = rope
        per_seq = n // tm
        body, name = _qkv_proj_kernel, "attn_qkv_proj"
        table = pl.BlockSpec((tm, LANES), lambda i: (jnp.where(i < tiles0, per_seq, (i - tiles0) % per_seq), 0))
        extra_specs, extra_args = [_layer(w, layer), table, table, table], [w, *tables]
    out_specs = [pl.BlockSpec((tm, w.shape[2]), lambda i: (i, 0))]
    out_shape = [jax.ShapeDtypeStruct((t, w.shape[2]), F32)]
    if lb is not None:
        out_specs.append(pl.BlockSpec((None, 1, d), lambda i: (i, 0, 0)))
        out_shape.append(jax.ShapeDtypeStruct((t // tm, 1, d), F32))
    outs = pl.pallas_call(
        functools.partial(body, d=d, tiles0=tiles0),
        grid=(t // tm,),
        in_specs=_split_specs(tm, d, tiles0) + [
            pl.BlockSpec((None, 1, mod.shape[-1]), lambda i: (cond_row(i * tm), 0, 0)),
            _resident((1, d)),
        ] + extra_specs,
        out_specs=out_specs,
        out_shape=out_shape,
        compiler_params=_params("parallel"),
        name=name,
    )(x0, x1, mod, gain, *extra_args)
    return outs if lb is not None else outs[0]


def _ffn_kernel(x0_ref, x1_ref, a0_ref, a1_ref, mod_ref, gain_ref, wo_ref, wgu_ref, wd_ref, nf_ref, *out_refs,
                d, d_ff, ff_chunk, final, tiles0):
    g1 = mod_ref[:, 2 * d:3 * d]
    sh2 = mod_ref[:, 3 * d:4 * d]
    sc2 = mod_ref[:, 4 * d:5 * d]
    g2 = mod_ref[:, 5 * d:6 * d]
    a = _load_split(a0_ref, a1_ref, tiles0)
    x1 = _load_split(x0_ref, x1_ref, tiles0) + g1 * _dot(a.astype(BF16), wo_ref[...])
    h = (_rmsnorm(x1, gain_ref[...]) * (1.0 + sc2) + sh2).astype(BF16)
    y = None
    for j in range(d_ff // ff_chunk):
        lo = j * ff_chunk
        g = _dot(h, wgu_ref[:, lo:lo + ff_chunk])
        u = _dot(h, wgu_ref[:, d_ff + lo:d_ff + lo + ff_chunk])
        part = _dot((_silu(g) * u).astype(BF16), wd_ref[lo:lo + ff_chunk, :])
        y = part if y is None else y + part
    x2 = x1 + g2 * y
    if final:
        x2 = _rmsnorm(x2, nf_ref[...])
    _store_split(out_refs[0], out_refs[1], tiles0, x2)


def _ffn_call(x, a, mod, gain, w_out, mixer_layer, w_gu, w_d, layer, norm_final, cond_row, tm, final):
    (x0, x1), (a0, a1) = x, a
    d = x0.shape[1]
    d_ff = w_d.shape[1]
    ff_chunk = d_ff
    tiles0 = x0.shape[0] // tm
    rows = _split_specs(tm, d, tiles0)
    return pl.pallas_call(
        functools.partial(_ffn_kernel, d=d, d_ff=d_ff, ff_chunk=ff_chunk, final=final, tiles0=tiles0),
        grid=((x0.shape[0] + x1.shape[0]) // tm,),
        in_specs=rows + rows + [
            pl.BlockSpec((None, 1, mod.shape[-1]), lambda i: (cond_row(i * tm), 0, 0)),
            _resident((1, d)),
            _layer(w_out, mixer_layer),
            _layer(w_gu, layer),
            _layer(w_d, layer),
            _resident((1, d)),
        ],
        out_specs=rows,
        out_shape=[jax.ShapeDtypeStruct(x0.shape, F32), jax.ShapeDtypeStruct(x1.shape, F32)],
        compiler_params=_params("arbitrary"),
        name="outproj_ffn",
    )(x0, x1, a0, a1, mod, gain, w_out, w_gu, w_d, norm_final)


def _row_blocks(b, rows, size):
    parts = [jnp.broadcast_to(b[r:r + 1, :], (size, b.shape[1])) for r in rows]
    return parts[0] if len(parts) == 1 else jnp.concatenate(parts, axis=0)


def _scores_blockwise(q, k, b, reverse):
    c, dk = q.shape
    blk = GLA_BLOCK
    nblk = c // blk
    zero_row = jnp.zeros((1, dk), F32)
    rows = []
    for i in range(nblk):
        lo, hi = i * blk, (i + 1) * blk
        if reverse:
            r = b[hi:hi + 1, :] if i < nblk - 1 else zero_row
            key_lo, key_hi = lo, c
        else:
            r = b[lo - 1:lo, :] if i > 0 else zero_row
            key_lo, key_hi = 0, hi
        qi = (q[lo:hi] * jnp.exp2(b[lo:hi] - r)).astype(BF16)
        ki = (k[key_lo:key_hi] * jnp.exp2(r - b[key_lo:key_hi])).astype(BF16)
        if key_hi - key_lo < c:
            pad = jnp.zeros((c - (key_hi - key_lo), dk), BF16)
            ki = jnp.concatenate([pad, ki] if reverse else [ki, pad], axis=0)
        rows.append(_dot_nt(qi, ki))
    return jnp.concatenate(rows, axis=0)


def _causal(c, reverse):
    t_idx = lax.broadcasted_iota(jnp.int32, (c, c), 0)
    s_idx = lax.broadcasted_iota(jnp.int32, (c, c), 1)
    return (s_idx >= t_idx) if reverse else (s_idx <= t_idx)


def _gla_decay_sums(lg, tri):
    hi = lg.astype(BF16)
    mid = (lg - hi.astype(F32)).astype(BF16)
    return _dot(jnp.where(tri, 1.0, 0.0).astype(BF16), jnp.concatenate([hi, mid], axis=1))


def _gla_mix(q, lg, b2, v, st, tri, reverse, bounded):
    c = q.shape[0]
    dk = q.shape[1]
    k = 1.0 - jnp.exp2(lg)
    b = b2[:, 0:dk] + b2[:, dk:2 * dk]
    b_end = b[0:1, :] if reverse else b[c - 1:c, :]
    o = _dot_nt((q * jnp.exp2(b)).astype(BF16), st.astype(BF16))
    k_dec = (k * jnp.exp2(b_end - b)).astype(BF16)
    st_new = st * jnp.exp2(b_end) + _dot(v.T.astype(BF16), k_dec)
    if bounded:
        scores = jnp.where(tri, _scores_blockwise(q, k, b, reverse), 0.0)
    else:
        scores = _scores_robust(q, k, b, tri, reverse)
    return o, st_new, scores.astype(BF16)


def _scores_robust(q, k, b, tri, reverse):
    c, dk = q.shape
    t_idx = lax.broadcasted_iota(jnp.int32, (c, c), 0)
    s_idx = lax.broadcasted_iota(jnp.int32, (c, c), 1)
    pos = lax.broadcasted_iota(jnp.int32, (c, dk), 0)
    scores = jnp.zeros((c, c), F32)
    m = c // 2
    while m >= GLA_LEAF:
        nblk = c // (2 * m)
        bound = [kb * 2 * m + (m if reverse else m - 1) for kb in range(nblk)]
        e = jnp.exp2(-jnp.abs(b - _row_blocks(b, bound, 2 * m)))
        late = _mod(pos, 2 * m) >= m
        q_side = jnp.logical_not(late) if reverse else late
        ql = jnp.where(q_side, q * e, 0.0).astype(BF16)
        kl = jnp.where(q_side, 0.0, k * e).astype(BF16)
        s_l = _dot_nt(ql, kl)
        if nblk > 1:
            s_l = jnp.where(_div(t_idx, 2 * m) == _div(s_idx, 2 * m), s_l, 0.0)
        scores = scores + s_l
        m //= 2
    nleaf = c // GLA_LEAF
    k3 = k.reshape(nleaf, GLA_LEAF, dk)
    b3 = b.reshape(nleaf, GLA_LEAF, dk)
    sel_d = _mod(lax.broadcasted_iota(jnp.int32, (dk, c), 1), GLA_LEAF)
    r = jnp.zeros((c, c), F32)
    for sl in range(GLA_LEAF):
        ks = jnp.broadcast_to(k3[:, sl:sl + 1, :], k3.shape).reshape(c, dk)
        bs = jnp.broadcast_to(b3[:, sl:sl + 1, :], b3.shape).reshape(c, dk)
        term = (q * ks * jnp.exp2(jnp.minimum(b - bs, 0.0))).astype(BF16)
        r = r + _dot(term, jnp.where(sel_d == sl, 1.0, 0.0).astype(BF16))
    same_leaf = _div(t_idx, GLA_LEAF) == _div(s_idx, GLA_LEAF)
    return scores + jnp.where(jnp.logical_and(same_leaf, tri), r, 0.0)


def _gla_kernel(*refs, n, hb, has_s0, nprev, want_state):
    q_ref, lgf_ref, lgb_ref, v_ref, g_ref, gn_ref, low_ref = refs[:7]
    pos = 7
    s0_ref = prev_ref = sfin_ref = None
    if has_s0:
        s0_ref = refs[pos]
        pos += 1
    if nprev:
        prev_ref = refs[pos]
        pos += 1
    o_ref = refs[pos]
    pos += 1
    if want_state:
        sfin_ref = refs[pos]
        pos += 1
    half_ref, st_ref = refs[pos:pos + 2]
    c = GLA_CHUNK
    hd = LANES
    nchunks = n // c

    for direction in (0, 1):
        for h in range(hb):
            if has_s0:
                st_ref[direction, h] = s0_ref[direction, h].T
            else:
                st_ref[direction, h] = jnp.zeros((hd, hd), F32)

    bounded = jnp.min(low_ref[...]) >= -GLA_RANGE_BOUND

    def scan(is_bounded):
        def body(i, carry, closing):
            tri = (_causal(c, False), _causal(c, True))
            chains = []
            for h in range(hb):
                cols = slice(h * hd, (h + 1) * hd)
                for direction in (0, 1):
                    ci = i if direction == 0 else nchunks - 1 - i
                    rows = pl.ds(pl.multiple_of(ci * c, c), c)
                    lg = (lgf_ref if direction == 0 else lgb_ref)[rows, cols]
                    chains.append((h, direction, rows, cols, lg, _gla_decay_sums(lg, tri[direction])))
            mixed = []
            for h, direction, rows, cols, lg, b2 in chains:
                mixed.append(_gla_mix(q_ref[rows, cols], lg, b2, v_ref[rows, cols], st_ref[direction, h],
                                      tri[direction], direction == 1, is_bounded))
            for (h, direction, rows, cols, _, _), (o, st_new, scores) in zip(chains, mixed):
                st_ref[direction, h] = st_new
                o = o + _dot(scores, v_ref[rows, cols].astype(BF16))
                if closing:
                    o = o + half_ref[rows, cols]
                    o = o * lax.rsqrt(jnp.mean(o * o, axis=-1, keepdims=True) + EPS) * gn_ref[:, cols]
                    o_ref[rows, cols] = o * g_ref[rows, cols]
                else:
                    half_ref[rows, cols] = o
            return carry

        lax.fori_loop(0, nchunks // 2, functools.partial(body, closing=False), 0)
        lax.fori_loop(nchunks // 2, nchunks, functools.partial(body, closing=True), 0)

    pl.when(bounded)(functools.partial(scan, True))
    pl.when(jnp.logical_not(bounded))(functools.partial(scan, False))

    if want_state:
        if nprev:
            sfin_ref[0:nprev] = prev_ref[...]
        for direction in (0, 1):
            for h in range(hb):
                sfin_ref[nprev, direction, h] = st_ref[direction, h].T


def _gla_call(proj, low, gnorm, s0, prev, row0, nseq, n, hb, want_state, name):
    hd = LANES
    heads = A_HEADS
    hblocks = heads // hb
    assert n % (2 * GLA_CHUNK) == 0
    blk0 = row0 // n

    def col(kind):
        return pl.BlockSpec((n, hb * hd), lambda b, h: (blk0 + b, kind * hblocks + h))

    in_specs = [col(0), col(1), col(2), col(3), col(4),
                pl.BlockSpec((1, hb * hd), lambda b, h: (0, h)),
                pl.BlockSpec((n // HGRN_PROJ_TILE, 1, hb * hd), lambda b, h: (blk0 + b, 0, h))]
    args = [proj, proj, proj, proj, proj, gnorm, low]
    state_spec = pl.BlockSpec((None, 2, hb, hd, hd), lambda b, h: (b, 0, h, 0, 0))
    if s0 is not None:
        in_specs.append(state_spec)
        args.append(s0)
    nprev = 0 if prev is None else prev.shape[1]

    def layered(layers):
        return pl.BlockSpec((None, layers, 2, hb, hd, hd), lambda b, h: (b, 0, 0, h, 0, 0))

    if nprev:
        in_specs.append(layered(nprev))
        args.append(prev)
    out_shape = [jax.ShapeDtypeStruct((nseq * n, heads * hd), F32)]
    out_specs = [pl.BlockSpec((n, hb * hd), lambda b, h: (b, h))]
    if want_state:
        out_shape.append(jax.ShapeDtypeStruct((nseq, nprev + 1, 2, heads, hd, hd), F32))
        out_specs.append(layered(nprev + 1))
    return pl.pallas_call(
        functools.partial(_gla_kernel, n=n, hb=hb, has_s0=s0 is not None, nprev=nprev, want_state=want_state),
        grid=(nseq, hblocks),
        in_specs=in_specs,
        out_specs=out_specs,
        out_shape=out_shape,
        scratch_shapes=[pltpu.VMEM((n, hb * hd), F32), pltpu.VMEM((2, hb, hd, hd), F32)],
        compiler_params=_params("parallel", "parallel"),
        name=name,
    )(*args)


def _stacked_queries(q_ref, kh, scale):
    heads = [q_ref[:, (kh * GROUP + g) * LANES:(kh * GROUP + g + 1) * LANES] for g in range(GROUP)]
    return (jnp.concatenate(heads, axis=0) * scale).astype(BF16)


def _sink_column(sink_ref, kh, rows):
    gi = _div(lax.broadcasted_iota(jnp.int32, (GROUP * rows, 1), 0), rows)
    col = jnp.zeros((GROUP * rows, 1), F32)
    for g in range(GROUP):
        col = jnp.where(gi == g, sink_ref[kh, g] * LOG2_E, col)
    return col


def _softmax_pv(logits, sinks, values):
    mx = [jnp.maximum(jnp.max(lg, axis=-1, keepdims=True), s) for lg, s in zip(logits, sinks)]
    p = [jnp.exp2(lg - m) for lg, m in zip(logits, mx)]
    den = [jnp.sum(pi, axis=-1, keepdims=True) + jnp.exp2(s - m) for pi, s, m in zip(p, sinks, mx)]
    return [_dot(pi.astype(BF16), v) * (1.0 / d) for pi, v, d in zip(p, values, den)]


def _store_heads(o_ref, kh, o):
    rows = o_ref.shape[0]
    for g in range(GROUP):
        o_ref[:, (kh * GROUP + g) * LANES:(kh * GROUP + g + 1) * LANES] = o[g * rows:(g + 1) * rows, :]


def _attn_ctx_kernel(sink_ref, q_ref, k_ref, v_ref, *rest, scale, nprev):
    if nprev:
        pk_ref, pv_ref, o_ref, nk_ref, nv_ref = rest
        nk_ref[0:nprev] = pk_ref[...]
        nv_ref[0:nprev] = pv_ref[...]
    else:
        o_ref, nk_ref, nv_ref = rest
    nk_ref[nprev] = k_ref[...]
    nv_ref[nprev] = v_ref[...]
    rows = q_ref.shape[0]
    heads = range(KV_HEADS)
    q = [_stacked_queries(q_ref, kh, scale) for kh in heads]
    logits = [_dot_nt(q[kh], k_ref[:, kh * LANES:(kh + 1) * LANES].astype(BF16)) for kh in heads]
    sinks = [_sink_column(sink_ref, kh, rows) for kh in heads]
    values = [v_ref[:, kh * LANES:(kh + 1) * LANES].astype(BF16) for kh in heads]
    for kh, o in zip(heads, _softmax_pv(logits, sinks, values)):
        _store_heads(o_ref, kh, o)


def _attn_ctx_call(qkv, sink, prev_k, prev_v, nseq, n):
    hd = LANES
    kvw = KV_HEADS * hd
    nprev = 0 if prev_k is None else prev_k.shape[1]

    def layered(layers):
        return pl.BlockSpec((None, layers, n, kvw), lambda b: (b, 0, 0, 0))

    kv_shape = jax.ShapeDtypeStruct((nseq, nprev + 1, n, kvw), F32)
    prev_specs = [layered(nprev), layered(nprev)] if nprev else []
    prev_args = [prev_k, prev_v] if nprev else []
    return pl.pallas_call(
        functools.partial(_attn_ctx_kernel, scale=hd ** -0.5 * LOG2_E, nprev=nprev),
        grid=(nseq,),
        in_specs=[
            pl.BlockSpec(memory_space=pltpu.SMEM),
            pl.BlockSpec((n, ATT_HEADS * hd), lambda b: (b, 0)),
            pl.BlockSpec((n, kvw), lambda b: (b, ATT_HEADS // KV_HEADS)),
            pl.BlockSpec((n, kvw), lambda b: (b, ATT_HEADS // KV_HEADS + 1)),
        ] + prev_specs,
        out_specs=[pl.BlockSpec((n, ATT_HEADS * hd), lambda b: (b, 0)), layered(nprev + 1), layered(nprev + 1)],
        out_shape=[jax.ShapeDtypeStruct((nseq * n, ATT_HEADS * hd), F32), kv_shape, kv_shape],
        compiler_params=_params("parallel"),
        name="attn_context",
    )(sink, qkv, qkv, qkv, *prev_args)


def _attn_lat_kernel(sink_ref, bias_ref, q_ref, k0_ref, k1_ref, k2_ref, v0_ref, v1_ref, v2_ref, kc_ref, vc_ref,
                     o_ref, *, scale, nblk):
    nb = pl.program_id(1)
    rows = q_ref.shape[0]
    past = kc_ref.shape[0]
    heads = range(KV_HEADS)
    bias_before = bias_ref[:, 0:BLOCK] + jnp.where(nb == 0, NEG_BIG, 0.0)
    bias_after = bias_ref[:, BLOCK:] + jnp.where(nb == nblk - 1, NEG_BIG, 0.0)

    def rows_of(kh, refs):
        return jnp.concatenate([r[:, kh * LANES:(kh + 1) * LANES] for r in refs], axis=0).astype(BF16)

    q = [_stacked_queries(q_ref, kh, scale) for kh in heads]
    raw = [_dot_nt(q[kh], rows_of(kh, (kc_ref, k0_ref, k1_ref, k2_ref))) for kh in heads]
    logits = [jnp.concatenate([lg[:, 0:past], lg[:, past:past + BLOCK] + bias_before,
                               lg[:, past + BLOCK:past + 2 * BLOCK], lg[:, past + 2 * BLOCK:] + bias_after], axis=1)
              for lg in raw]
    sinks = [_sink_column(sink_ref, kh, rows) for kh in heads]
    values = [rows_of(kh, (vc_ref, v0_ref, v1_ref, v2_ref)) for kh in heads]
    for kh, o in zip(heads, _softmax_pv(logits, sinks, values)):
        _store_heads(o_ref, kh, o)


def _attn_lat_call(qkv, cache_k, cache_v, sink, row0, nseq, n):
    hd = LANES
    nblk = n // BLOCK
    past = cache_k.shape[1]
    kcol = ATT_HEADS // KV_HEADS
    blk0 = row0 // BLOCK

    def near(off, colblock):
        def index(b, i):
            return (blk0 + b * nblk + jnp.clip(i + off, 0, nblk - 1), colblock)
        return pl.BlockSpec((BLOCK, KV_HEADS * hd), index)

    t_in = jnp.arange(GROUP * BLOCK)[:, None] % BLOCK
    j_in = jnp.arange(BLOCK)[None, :]
    bias = jnp.concatenate([jnp.where(j_in >= t_in, 0.0, NEG_BIG), jnp.where(j_in <= t_in, 0.0, NEG_BIG)],
                           axis=1).astype(F32)
    cache_spec = pl.BlockSpec((None, past, KV_HEADS * hd), lambda b, i: (b, 0, 0))
    return pl.pallas_call(
        functools.partial(_attn_lat_kernel, scale=hd ** -0.5 * LOG2_E, nblk=nblk),
        grid=(nseq, nblk),
        in_specs=[
            pl.BlockSpec(memory_space=pltpu.SMEM),
            pl.BlockSpec(bias.shape, lambda b, i: (0, 0)),
            pl.BlockSpec((BLOCK, ATT_HEADS * hd), lambda b, i: (blk0 + b * nblk + i, 0)),
            near(-1, kcol), near(0, kcol), near(1, kcol),
            near(-1, kcol + 1), near(0, kcol + 1), near(1, kcol + 1),
            cache_spec, cache_spec,
        ],
        out_specs=pl.BlockSpec((BLOCK, ATT_HEADS * hd), lambda b, i: (b * nblk + i, 0)),
        out_shape=jax.ShapeDtypeStruct((nseq * n, ATT_HEADS * hd), F32),
        compiler_params=_params("parallel", "parallel"),
        name="attn_latent",
    )(sink, bias, qkv, qkv, qkv, qkv, qkv, qkv, qkv, cache_k, cache_v)


def _rope_tables(n, pad):
    quarter = LANES // 4
    inv = ROPE_BASE ** (-jnp.arange(quarter, dtype=F32) / quarter)
    pos = jnp.arange(n)
    row_pos = (pos // GRID_W).astype(F32)
    col_pos = (pos % GRID_W).astype(F32)
    ang_r = row_pos[:, None] * inv[None, :]
    ang_c = col_pos[:, None] * inv[None, :]
    zero = jnp.zeros_like(ang_r)
    cos = jnp.concatenate([jnp.cos(ang_r), jnp.cos(ang_r), jnp.cos(ang_c), jnp.cos(ang_c)], axis=1)
    sin_up = jnp.concatenate([-jnp.sin(ang_r), zero, -jnp.sin(ang_c), zero], axis=1)
    sin_dn = jnp.concatenate([zero, jnp.sin(ang_r), zero, jnp.sin(ang_c)], axis=1)
    ones, zeros = jnp.ones((pad, LANES), F32), jnp.zeros((pad, LANES), F32)
    return (jnp.concatenate([cos, ones]), jnp.concatenate([sin_up, zeros]), jnp.concatenate([sin_dn, zeros]))


def _lower_bounds(lb_param):
    p = jax.nn.softmax(lb_param.astype(F32), axis=0)
    cs = jnp.cumsum(p, axis=0)
    return cs - cs[:1]


def kernel(x_prompt, x_sample, cache_k, cache_v, state_hgrn, c, c_ctx, w_ada, b_ada, norm1, norm2, norm_final,
           w_gate_up, w_down, w_in_a, lower_bounds, gnorm_a, w_out_a, w_qkv_b, w_out_b, sink_b):
    batch, seq, d = x_prompt.shape
    dec_batch, dec_seq, _ = x_sample.shape
    depth = w_ada.shape[0]
    t_ctx = batch * seq
    t_lat = dec_batch * dec_seq
    assert dec_batch + 1 <= SUBLANES and t_ctx % TOKEN_TILE == 0 and dec_seq % TOKEN_TILE == 0
    assert WINDOW == BLOCK

    def cond_row(tok0):
        return jnp.where(tok0 < t_ctx, 0, 1 + (tok0 - t_ctx) // dec_seq)

    x = (x_prompt.reshape(t_ctx, d), x_sample.reshape(t_lat, d))
    cond8 = jnp.concatenate([c_ctx[None, :], c, jnp.zeros((SUBLANES - 1 - dec_batch, d), F32)], axis=0)
    mod = _ada_call(cond8, w_ada, b_ada).reshape(depth, SUBLANES, 1, 6 * d)
    lb_all = _lower_bounds(lower_bounds)
    rope_tabs = _rope_tables(dec_seq, TOKEN_TILE)
    past = cache_k.shape[2]
    nf = norm_final.reshape(1, d)

    w_in_a, w_qkv_b, w_out_a, w_out_b, w_gate_up, w_down = (
        w.astype(BF16) for w in (w_in_a, w_qkv_b, w_out_a, w_out_b, w_gate_up, w_down))
    new_k = new_v = new_s = None
    for l in range(depth):
        j = l // 2
        n1 = norm1[l].reshape(1, d)
        n2 = norm2[l].reshape(1, d)
        if l % 2 == 0:
            proj, low = _proj_call(x, mod[l], n1, w_in_a, j, cond_row, HGRN_PROJ_TILE,
                                   lb_all[j].reshape(1, 2 * d))
            gn = gnorm_a[j].reshape(1, d)
            a_ctx, new_s = _gla_call(proj, low, gn, None, new_s, 0, batch, seq, GLA_HEADS_CTX, True,
                                     "hgrn_scan_ctx")
            (a_lat,) = _gla_call(proj, low, gn, state_hgrn[:, j], None, t_ctx, dec_batch, dec_seq, GLA_HEADS_LAT,
                                 False, "hgrn_scan_lat")
            w_mix = w_out_a
        else:
            qkv = _proj_call(x, mod[l], n1, w_qkv_b, j, cond_row, TOKEN_TILE, rope=(rope_tabs, dec_seq))
            sink = sink_b[j].reshape(KV_HEADS, GROUP)
            a_ctx, new_k, new_v = _attn_ctx_call(qkv, sink, new_k, new_v, batch, seq)
            a_lat = _attn_lat_call(qkv, cache_k[:, j].reshape(dec_batch, past, KV_HEADS * LANES),
                                   cache_v[:, j].reshape(dec_batch, past, KV_HEADS * LANES), sink,
                                   t_ctx, dec_batch, dec_seq)
            w_mix = w_out_b
        x = _ffn_call(x, (a_ctx, a_lat), mod[l], n2, w_mix, j, w_gate_up, w_down, l, nf, cond_row, TOKEN_TILE,
                      l == depth - 1)

    y_prompt = x[0].reshape(batch, seq, d)
    y_sample = x[1].reshape(dec_batch, dec_seq, d)
    kv_shape = (batch, new_k.shape[1], seq, KV_HEADS, LANES)
    return (y_prompt, y_sample, new_k.reshape(kv_shape), new_v.reshape(kv_shape), new_s)
```

```python
import functools

import jax
import jax.numpy as jnp
from jax import lax
from jax.experimental import pallas as pl
from jax.experimental.pallas import tpu as pltpu

F32 = jnp.float32
BF16 = jnp.bfloat16
EPS = 1e-6
ROPE_BASE = 10000.0

V7X_VMEM_LIMIT_BYTES = 56 * 1024 * 1024
SUBLANES = 8
LANES = 128

A_HEADS = 8
ATT_HEADS = 8
KV_HEADS = 2
GROUP = ATT_HEADS // KV_HEADS
WINDOW = 128
BLOCK = 128
GRID_W = 64
TOKEN_TILE = 512
HGRN_PROJ_TILE = 256
GLA_CHUNK = 128
GLA_LEAF = SUBLANES
GLA_HEADS_CTX = 8
GLA_HEADS_LAT = 4
GLA_VMEM_HEADROOM_BYTES = 8 * 1024 * 1024
GLA_BLOCK = 32
GLA_RANGE_BOUND = 86.0
LOG2_E = 1.4426950408889634
NEG_BIG = -1e30


def _params(*sem):
    return pltpu.CompilerParams(dimension_semantics=sem, vmem_limit_bytes=V7X_VMEM_LIMIT_BYTES)


def _resident(shape):
    nd = len(shape)
    return pl.BlockSpec(shape, lambda *_: (0,) * nd, pipeline_mode=pl.Buffered(1))


def _layer(stacked, layer):
    return pl.BlockSpec((None,) + stacked.shape[1:], lambda *_: (layer, 0, 0), pipeline_mode=pl.Buffered(1))


def _div(x, k):
    assert k & (k - 1) == 0
    return jnp.right_shift(x, k.bit_length() - 1)


def _mod(x, k):
    assert k & (k - 1) == 0
    return jnp.bitwise_and(x, k - 1)


def _sigmoid(x):
    return 1.0 / (1.0 + jnp.exp(-x))


def _silu(x):
    return x * _sigmoid(x)


def _rmsnorm(x, gain):
    return x * lax.rsqrt(jnp.mean(x * x, axis=-1, keepdims=True) + EPS) * gain


def _dot(a, b):
    return jnp.dot(a, b, preferred_element_type=F32)


def _dot_nt(a, b):
    return lax.dot_general(a, b, (((1,), (1,)), ((), ())), preferred_element_type=F32)


def _ada_kernel(cond_ref, w_ref, b_ref, out_ref):
    s = _silu(cond_ref[...]).astype(BF16)
    out_ref[...] = _dot(s, w_ref[...].astype(BF16)) + b_ref[...]


def _ada_call(cond8, w_ada, b_ada):
    depth, d, n = w_ada.shape
    tn = n // 4
    return pl.pallas_call(
        _ada_kernel,
        grid=(depth, n // tn),
        in_specs=[
            pl.BlockSpec((SUBLANES, d), lambda l, j: (0, 0)),
            pl.BlockSpec((None, d, tn), lambda l, j: (l, 0, j)),
            pl.BlockSpec((None, 1, tn), lambda l, j: (l, 0, j)),
        ],
        out_specs=pl.BlockSpec((None, SUBLANES, tn), lambda l, j: (l, 0, j)),
        out_shape=jax.ShapeDtypeStruct((depth, SUBLANES, n), F32),
        compiler_params=_params("parallel", "parallel"),
        name="adaln_mod",
    )(cond8, w_ada, b_ada.reshape(depth, 1, n))


def _split_specs(tm, width, tiles0):
    return [pl.BlockSpec((tm, width), lambda i: (jnp.minimum(i, tiles0 - 1), 0)),
            pl.BlockSpec((tm, width), lambda i: (jnp.maximum(i - tiles0, 0), 0))]


def _load_split(ref0, ref1, tiles0):
    return jnp.where(pl.program_id(0) < tiles0, ref0[...], ref1[...])


def _store_split(ref0, ref1, tiles0, value):
    @pl.when(pl.program_id(0) < tiles0)
    def _():
        ref0[...] = value

    @pl.when(pl.program_id(0) >= tiles0)
    def _():
        ref1[...] = value


def _modulated(x0_ref, x1_ref, mod_ref, gain_ref, tiles0, d):
    shift = mod_ref[:, 0:d]
    scale = mod_ref[:, d:2 * d]
    x = _load_split(x0_ref, x1_ref, tiles0)
    return (_rmsnorm(x, gain_ref[...]) * (1.0 + scale) + shift).astype(BF16)


def _qkv_proj_kernel(x0_ref, x1_ref, mod_ref, gain_ref, w_ref, cos_ref, sa_ref, sb_ref, out_ref, *, d, tiles0):
    p = _dot(_modulated(x0_ref, x1_ref, mod_ref, gain_ref, tiles0, d), w_ref[...])
    cos = cos_ref[...]
    sa = sa_ref[...]
    sb = sb_ref[...]
    for h in range(p.shape[1] // LANES):
        x = p[:, h * LANES:(h + 1) * LANES]
        if h < ATT_HEADS + KV_HEADS:
            x = x * cos + pltpu.roll(x, LANES - 32, 1) * sa + pltpu.roll(x, 32, 1) * sb
        out_ref[:, h * LANES:(h + 1) * LANES] = x


def _hgrn_proj_kernel(x0_ref, x1_ref, mod_ref, gain_ref, lb_ref, w_ref, out_ref, low_ref, *, d, tiles0):
    h = _modulated(x0_ref, x1_ref, mod_ref, gain_ref, tiles0, d)
    low = None
    for part in range(5):
        cols = slice(part * d, (part + 1) * d)
        p = _dot(h, w_ref[:, cols])
        if part in (0, 4):
            p = _silu(p)
        elif part in (1, 2):
            lb = lb_ref[:, (part - 1) * d:part * d]
            p = jnp.log2(lb + (1.0 - lb) * _sigmoid(p))
            blocks = jnp.sum(p.reshape(p.shape[0] // GLA_BLOCK, GLA_BLOCK, d), axis=1)
            part_low = jnp.min(blocks, axis=0, keepdims=True)
            low = part_low if low is None else jnp.minimum(low, part_low)
        out_ref[:, cols] = p
    low_ref[...] = low


def _proj_call(x, mod, gain, w, layer, cond_row, tm, lb=None, rope=None):
    x0, x1 = x
    d = x0.shape[1]
    t = x0.shape[0] + x1.shape[0]
    tiles0 = x0.shape[0] // tm
    if lb is not None:
        body, name = _hgrn_proj_kernel, "hgrn_in_proj"
        extra_specs, extra_args = [_resident((1, 2 * d)), _layer(w, layer)], [lb, w]
    else:
        tables, n = rope
        per_seq = n // tm
        body, name = _qkv_proj_kernel, "attn_qkv_proj"
        table = pl.BlockSpec((tm, LANES), lambda i: (jnp.where(i < tiles0, per_seq, (i - tiles0) % per_seq), 0))
        extra_specs, extra_args = [_layer(w, layer), table, table, table], [w, *tables]
    out_specs = [pl.BlockSpec((tm, w.shape[2]), lambda i: (i, 0))]
    out_shape = [jax.ShapeDtypeStruct((t, w.shape[2]), F32)]
    if lb is not None:
        out_specs.append(pl.BlockSpec((None, 1, d), lambda i: (i, 0, 0)))
        out_shape.append(jax.ShapeDtypeStruct((t // tm, 1, d), F32))
    outs = pl.pallas_call(
        functools.partial(body, d=d, tiles0=tiles0),
        grid=(t // tm,),
        in_specs=_split_specs(tm, d, tiles0) + [
            pl.BlockSpec((None, 1, mod.shape[-1]), lambda i: (cond_row(i * tm), 0, 0)),
            _resident((1, d)),
        ] + extra_specs,
        out_specs=out_specs,
        out_shape=out_shape,
        compiler_params=_params("parallel"),
        name=name,
    )(x0, x1, mod, gain, *extra_args)
    return outs if lb is not None else outs[0]


def _ffn_kernel(x0_ref, x1_ref, a0_ref, a1_ref, mod_ref, gain_ref, wo_ref, wgu_ref, wd_ref, nf_ref, *out_refs,
                d, d_ff, ff_chunk, final, tiles0):
    g1 = mod_ref[:, 2 * d:3 * d]
    sh2 = mod_ref[:, 3 * d:4 * d]
    sc2 = mod_ref[:, 4 * d:5 * d]
    g2 = mod_ref[:, 5 * d:6 * d]
    a = _load_split(a0_ref, a1_ref, tiles0)
    x1 = _load_split(x0_ref, x1_ref, tiles0) + g1 * _dot(a.astype(BF16), wo_ref[...])
    h = (_rmsnorm(x1, gain_ref[...]) * (1.0 + sc2) + sh2).astype(BF16)
    y = None
    for j in range(d_ff // ff_chunk):
        lo = j * ff_chunk
        g = _dot(h, wgu_ref[:, lo:lo + ff_chunk])
        u = _dot(h, wgu_ref[:, d_ff + lo:d_ff + lo + ff_chunk])
        part = _dot((_silu(g) * u).astype(BF16), wd_ref[lo:lo + ff_chunk, :])
        y = part if y is None else y + part
    x2 = x1 + g2 * y
    if final:
        x2 = _rmsnorm(x2, nf_ref[...])
    _store_split(out_refs[0], out_refs[1], tiles0, x2)


def _ffn_call(x, a, mod, gain, w_out, mixer_layer, w_gu, w_d, layer, norm_final, cond_row, tm, final):
    (x0, x1), (a0, a1) = x, a
    d = x0.shape[1]
    d_ff = w_d.shape[1]
    ff_chunk = d_ff
    tiles0 = x0.shape[0] // tm
    rows = _split_specs(tm, d, tiles0)
    return pl.pallas_call(
        functools.partial(_ffn_kernel, d=d, d_ff=d_ff, ff_chunk=ff_chunk, final=final, tiles0=tiles0),
        grid=((x0.shape[0] + x1.shape[0]) // tm,),
        in_specs=rows + rows + [
            pl.BlockSpec((None, 1, mod.shape[-1]), lambda i: (cond_row(i * tm), 0, 0)),
            _resident((1, d)),
            _layer(w_out, mixer_layer),
            _layer(w_gu, layer),
            _layer(w_d, layer),
            _resident((1, d)),
        ],
        out_specs=rows,
        out_shape=[jax.ShapeDtypeStruct(x0.shape, F32), jax.ShapeDtypeStruct(x1.shape, F32)],
        compiler_params=_params("arbitrary"),
        name="outproj_ffn",
    )(x0, x1, a0, a1, mod, gain, w_out, w_gu, w_d, norm_final)


def _row_blocks(b, rows, size):
    parts = [jnp.broadcast_to(b[r:r + 1, :], (size, b.shape[1])) for r in rows]
    return parts[0] if len(parts) == 1 else jnp.concatenate(parts, axis=0)


def _scores_blockwise(q, k, b, reverse):
    c, dk = q.shape
    blk = GLA_BLOCK
    nblk = c // blk
    zero_row = jnp.zeros((1, dk), F32)
    rows = []
    for i in range(nblk):
        lo, hi = i * blk, (i + 1) * blk
        if reverse:
            r = b[hi:hi + 1, :] if i < nblk - 1 else zero_row
            key_lo, key_hi = lo, c
        else:
            r = b[lo - 1:lo, :] if i > 0 else zero_row
            key_lo, key_hi = 0, hi
        qi = (q[lo:hi] * jnp.exp2(b[lo:hi] - r)).astype(BF16)
        ki = (k[key_lo:key_hi] * jnp.exp2(r - b[key_lo:key_hi])).astype(BF16)
        if key_hi - key_lo < c:
            pad = jnp.zeros((c - (key_hi - key_lo), dk), BF16)
            ki = jnp.concatenate([pad, ki] if reverse else [ki, pad], axis=0)
        rows.append(_dot_nt(qi, ki))
    return jnp.concatenate(rows, axis=0)


def _causal(c, reverse):
    t_idx = lax.broadcasted_iota(jnp.int32, (c, c), 0)
    s_idx = lax.broadcasted_iota(jnp.int32, (c, c), 1)
    return (s_idx >= t_idx) if reverse else (s_idx <= t_idx)


def _gla_decay_sums(lg, tri):
    hi = lg.astype(BF16)
    mid = (lg - hi.astype(F32)).astype(BF16)
    return _dot(jnp.where(tri, 1.0, 0.0).astype(BF16), jnp.concatenate([hi, mid], axis=1))


def _gla_mix(q, lg, b2, v, st, tri, reverse, bounded):
    c = q.shape[0]
    dk = q.shape[1]
    k = 1.0 - jnp.exp2(lg)
    b = b2[:, 0:dk] + b2[:, dk:2 * dk]
    b_end = b[0:1, :] if reverse else b[c - 1:c, :]
    o = _dot_nt((q * jnp.exp2(b)).astype(BF16), st.astype(BF16))
    k_dec = (k * jnp.exp2(b_end - b)).astype(BF16)
    st_new = st * jnp.exp2(b_end) + _dot(v.T.astype(BF16), k_dec)
    if bounded:
        scores = jnp.where(tri, _scores_blockwise(q, k, b, reverse), 0.0)
    else:
        scores = _scores_robust(q, k, b, tri, reverse)
    return o, st_new, scores.astype(BF16)


def _scores_robust(q, k, b, tri, reverse):
    c, dk = q.shape
    t_idx = lax.broadcasted_iota(jnp.int32, (c, c), 0)
    s_idx = lax.broadcasted_iota(jnp.int32, (c, c), 1)
    pos = lax.broadcasted_iota(jnp.int32, (c, dk), 0)
    scores = jnp.zeros((c, c), F32)
    m = c // 2
    while m >= GLA_LEAF:
        nblk = c // (2 * m)
        bound = [kb * 2 * m + (m if reverse else m - 1) for kb in range(nblk)]
        e = jnp.exp2(-jnp.abs(b - _row_blocks(b, bound, 2 * m)))
        late = _mod(pos, 2 * m) >= m
        q_side = jnp.logical_not(late) if reverse else late
        ql = jnp.where(q_side, q * e, 0.0).astype(BF16)
        kl = jnp.where(q_side, 0.0, k * e).astype(BF16)
        s_l = _dot_nt(ql, kl)
        if nblk > 1:
            s_l = jnp.where(_div(t_idx, 2 * m) == _div(s_idx, 2 * m), s_l, 0.0)
        scores = scores + s_l
        m //= 2
    nleaf = c // GLA_LEAF
    k3 = k.reshape(nleaf, GLA_LEAF, dk)
    b3 = b.reshape(nleaf, GLA_LEAF, dk)
    sel_d = _mod(lax.broadcasted_iota(jnp.int32, (dk, c), 1), GLA_LEAF)
    r = jnp.zeros((c, c), F32)
    for sl in range(GLA_LEAF):
        ks = jnp.broadcast_to(k3[:, sl:sl + 1, :], k3.shape).reshape(c, dk)
        bs = jnp.broadcast_to(b3[:, sl:sl + 1, :], b3.shape).reshape(c, dk)
        term = (q * ks * jnp.exp2(jnp.minimum(b - bs, 0.0))).astype(BF16)
        r = r + _dot(term, jnp.where(sel_d == sl, 1.0, 0.0).astype(BF16))
    same_leaf = _div(t_idx, GLA_LEAF) == _div(s_idx, GLA_LEAF)
    return scores + jnp.where(jnp.logical_and(same_leaf, tri), r, 0.0)


def _gla_kernel(*refs, n, hb, has_s0, nprev, want_state):
    q_ref, lgf_ref, lgb_ref, v_ref, g_ref, gn_ref, low_ref = refs[:7]
    pos = 7
    s0_ref = prev_ref = sfin_ref = None
    if has_s0:
        s0_ref = refs[pos]
        pos += 1
    if nprev:
        prev_ref = refs[pos]
        pos += 1
    o_ref = refs[pos]
    pos += 1
    if want_state:
        sfin_ref = refs[pos]
        pos += 1
    half_ref, st_ref = refs[pos:pos + 2]
    c = GLA_CHUNK
    hd = LANES
    nchunks = n // c

    for direction in (0, 1):
        for h in range(hb):
            if has_s0:
                st_ref[direction, h] = s0_ref[direction, h].T
            else:
                st_ref[direction, h] = jnp.zeros((hd, hd), F32)

    bounded = jnp.min(low_ref[...]) >= -GLA_RANGE_BOUND

    def scan(is_bounded):
        def body(i, carry, closing):
            tri = (_causal(c, False), _causal(c, True))
            chains = []
            for h in range(hb):
                cols = slice(h * hd, (h + 1) * hd)
                for direction in (0, 1):
                    ci = i if direction == 0 else nchunks - 1 - i
                    rows = pl.ds(pl.multiple_of(ci * c, c), c)
                    lg = (lgf_ref if direction == 0 else lgb_ref)[rows, cols]
                    chains.append((h, direction, rows, cols, lg, _gla_decay_sums(lg, tri[direction])))
            mixed = []
            for h, direction, rows, cols, lg, b2 in chains:
                mixed.append(_gla_mix(q_ref[rows, cols], lg, b2, v_ref[rows, cols], st_ref[direction, h],
                                      tri[direction], direction == 1, is_bounded))
            for (h, direction, rows, cols, _, _), (o, st_new, scores) in zip(chains, mixed):
                st_ref[direction, h] = st_new
                o = o + _dot(scores, v_ref[rows, cols].astype(BF16))
                if closing:
                    o = o + half_ref[rows, cols]
                    o = o * lax.rsqrt(jnp.mean(o * o, axis=-1, keepdims=True) + EPS) * gn_ref[:, cols]
                    o_ref[rows, cols] = o * g_ref[rows, cols]
                else:
                    half_ref[rows, cols] = o
            return carry

        lax.fori_loop(0, nchunks // 2, functools.partial(body, closing=False), 0)
        lax.fori_loop(nchunks // 2, nchunks, functools.partial(body, closing=True), 0)

    pl.when(bounded)(functools.partial(scan, True))
    pl.when(jnp.logical_not(bounded))(functools.partial(scan, False))

    if want_state:
        if nprev:
            sfin_ref[0:nprev] = prev_ref[...]
        for direction in (0, 1):
            for h in range(hb):
                sfin_ref[nprev, direction, h] = st_ref[direction, h].T


def _gla_call(proj, low, gnorm, s0, prev, row0, nseq, n, hb, want_state, name):
    hd = LANES
    heads = A_HEADS
    hblocks = heads // hb
    assert n % (2 * GLA_CHUNK) == 0
    blk0 = row0 // n

    block_bytes = n * hb * hd * 4
    tight = 2 * 6 * block_bytes + block_bytes > V7X_VMEM_LIMIT_BYTES - GLA_VMEM_HEADROOM_BYTES

    def col(kind):
        mode = {"pipeline_mode": pl.Buffered(1)} if tight and kind >= 3 else {}
        return pl.BlockSpec((n, hb * hd), lambda b, h: (blk0 + b, kind * hblocks + h), **mode)

    in_specs = [col(0), col(1), col(2), col(3), col(4),
                pl.BlockSpec((1, hb * hd), lambda b, h: (0, h)),
                pl.BlockSpec((n // HGRN_PROJ_TILE, 1, hb * hd), lambda b, h: (blk0 + b, 0, h))]
    args = [proj, proj, proj, proj, proj, gnorm, low]
    state_spec = pl.BlockSpec((None, 2, hb, hd, hd), lambda b, h: (b, 0, h, 0, 0))
    if s0 is not None:
        in_specs.append(state_spec)
        args.append(s0)
    nprev = 0 if prev is None else prev.shape[1]

    def layered(layers):
        return pl.BlockSpec((None, layers, 2, hb, hd, hd), lambda b, h: (b, 0, 0, h, 0, 0))

    if nprev:
        in_specs.append(layered(nprev))
        args.append(prev)
    out_shape = [jax.ShapeDtypeStruct((nseq * n, heads * hd), F32)]
    out_specs = [pl.BlockSpec((n, hb * hd), lambda b, h: (b, h))]
    if want_state:
        out_shape.append(jax.ShapeDtypeStruct((nseq, nprev + 1, 2, heads, hd, hd), F32))
        out_specs.append(layered(nprev + 1))
    return pl.pallas_call(
        functools.partial(_gla_kernel, n=n, hb=hb, has_s0=s0 is not None, nprev=nprev, want_state=want_state),
        grid=(nseq, hblocks),
        in_specs=in_specs,
        out_specs=out_specs,
        out_shape=out_shape,
        scratch_shapes=[pltpu.VMEM((n, hb * hd), F32), pltpu.VMEM((2, hb, hd, hd), F32)],
        compiler_params=_params("parallel", "parallel"),
        name=name,
    )(*args)


def _stacked_queries(q_ref, kh, scale):
    heads = [q_ref[:, (kh * GROUP + g) * LANES:(kh * GROUP + g + 1) * LANES] for g in range(GROUP)]
    return (jnp.concatenate(heads, axis=0) * scale).astype(BF16)


def _sink_column(sink_ref, kh, rows):
    gi = _div(lax.broadcasted_iota(jnp.int32, (GROUP * rows, 1), 0), rows)
    col = jnp.zeros((GROUP * rows, 1), F32)
    for g in range(GROUP):
        col = jnp.where(gi == g, sink_ref[kh, g] * LOG2_E, col)
    return col


def _softmax_pv(logits, sinks, values):
    mx = [jnp.maximum(jnp.max(lg, axis=-1, keepdims=True), s) for lg, s in zip(logits, sinks)]
    p = [jnp.exp2(lg - m) for lg, m in zip(logits, mx)]
    den = [jnp.sum(pi, axis=-1, keepdims=True) + jnp.exp2(s - m) for pi, s, m in zip(p, sinks, mx)]
    return [_dot(pi.astype(BF16), v) * (1.0 / d) for pi, v, d in zip(p, values, den)]


def _store_heads(o_ref, kh, o):
    rows = o_ref.shape[0]
    for g in range(GROUP):
        o_ref[:, (kh * GROUP + g) * LANES:(kh * GROUP + g + 1) * LANES] = o[g * rows:(g + 1) * rows, :]


def _attn_ctx_kernel(sink_ref, q_ref, k_ref, v_ref, *rest, scale, nprev):
    if nprev:
        pk_ref, pv_ref, o_ref, nk_ref, nv_ref = rest
        nk_ref[0:nprev] = pk_ref[...]
        nv_ref[0:nprev] = pv_ref[...]
    else:
        o_ref, nk_ref, nv_ref = rest
    for kh in range(KV_HEADS):
        nk_ref[nprev, :, kh, :] = k_ref[:, kh * LANES:(kh + 1) * LANES]
        nv_ref[nprev, :, kh, :] = v_ref[:, kh * LANES:(kh + 1) * LANES]
    rows = q_ref.shape[0]
    heads = range(KV_HEADS)
    q = [_stacked_queries(q_ref, kh, scale) for kh in heads]
    logits = [_dot_nt(q[kh], k_ref[:, kh * LANES:(kh + 1) * LANES].astype(BF16)) for kh in heads]
    sinks = [_sink_column(sink_ref, kh, rows) for kh in heads]
    values = [v_ref[:, kh * LANES:(kh + 1) * LANES].astype(BF16) for kh in heads]
    for kh, o in zip(heads, _softmax_pv(logits, sinks, values)):
        _store_heads(o_ref, kh, o)


def _attn_ctx_call(qkv, sink, prev_k, prev_v, nseq, n):
    hd = LANES
    kvw = KV_HEADS * hd
    nprev = 0 if prev_k is None else prev_k.shape[1]

    def layered(layers):
        return pl.BlockSpec((None, layers, n, KV_HEADS, hd), lambda b: (b, 0, 0, 0, 0))

    kv_shape = jax.ShapeDtypeStruct((nseq, nprev + 1, n, KV_HEADS, hd), F32)
    prev_specs = [layered(nprev), layered(nprev)] if nprev else []
    prev_args = [prev_k, prev_v] if nprev else []
    return pl.pallas_call(
        functools.partial(_attn_ctx_kernel, scale=hd ** -0.5 * LOG2_E, nprev=nprev),
        grid=(nseq,),
        in_specs=[
            pl.BlockSpec(memory_space=pltpu.SMEM),
            pl.BlockSpec((n, ATT_HEADS * hd), lambda b: (b, 0)),
            pl.BlockSpec((n, kvw), lambda b: (b, ATT_HEADS // KV_HEADS)),
            pl.BlockSpec((n, kvw), lambda b: (b, ATT_HEADS // KV_HEADS + 1)),
        ] + prev_specs,
        out_specs=[pl.BlockSpec((n, ATT_HEADS * hd), lambda b: (b, 0)), layered(nprev + 1), layered(nprev + 1)],
        out_shape=[jax.ShapeDtypeStruct((nseq * n, ATT_HEADS * hd), F32), kv_shape, kv_shape],
        compiler_params=_params("parallel"),
        name="attn_context",
    )(sink, qkv, qkv, qkv, *prev_args)


def _attn_lat_kernel(sink_ref, bias_ref, q_ref, k0_ref, k1_ref, k2_ref, v0_ref, v1_ref, v2_ref, kc_ref, vc_ref,
                     o_ref, *, scale, nblk):
    nb = pl.program_id(1)
    rows = q_ref.shape[0]
    past = kc_ref.shape[0]
    heads = range(KV_HEADS)
    bias_before = bias_ref[:, 0:BLOCK] + jnp.where(nb == 0, NEG_BIG, 0.0)
    bias_after = bias_ref[:, BLOCK:] + jnp.where(nb == nblk - 1, NEG_BIG, 0.0)

    def rows_of(kh, refs):
        return jnp.concatenate([r[:, kh * LANES:(kh + 1) * LANES] for r in refs], axis=0).astype(BF16)

    q = [_stacked_queries(q_ref, kh, scale) for kh in heads]
    raw = [_dot_nt(q[kh], rows_of(kh, (kc_ref, k0_ref, k1_ref, k2_ref))) for kh in heads]
    logits = [jnp.concatenate([lg[:, 0:past], lg[:, past:past + BLOCK] + bias_before,
                               lg[:, past + BLOCK:past + 2 * BLOCK], lg[:, past + 2 * BLOCK:] + bias_after], axis=1)
              for lg in raw]
    sinks = [_sink_column(sink_ref, kh, rows) for kh in heads]
    values = [rows_of(kh, (vc_ref, v0_ref, v1_ref, v2_ref)) for kh in heads]
    for kh, o in zip(heads, _softmax_pv(logits, sinks, values)):
        _store_heads(o_ref, kh, o)


def _attn_lat_call(qkv, cache_k, cache_v, sink, row0, nseq, n):
    hd = LANES
    nblk = n // BLOCK
    past = cache_k.shape[1]
    kcol = ATT_HEADS // KV_HEADS
    blk0 = row0 // BLOCK

    def near(off, colblock):
        def index(b, i):
            return (blk0 + b * nblk + jnp.clip(i + off, 0, nblk - 1), colblock)
        return pl.BlockSpec((BLOCK, KV_HEADS * hd), index)

    t_in = jnp.arange(GROUP * BLOCK)[:, None] % BLOCK
    j_in = jnp.arange(BLOCK)[None, :]
    bias = jnp.concatenate([jnp.where(j_in >= t_in, 0.0, NEG_BIG), jnp.where(j_in <= t_in, 0.0, NEG_BIG)],
                           axis=1).astype(F32)
    cache_spec = pl.BlockSpec((None, past, KV_HEADS * hd), lambda b, i: (b, 0, 0))
    return pl.pallas_call(
        functools.partial(_attn_lat_kernel, scale=hd ** -0.5 * LOG2_E, nblk=nblk),
        grid=(nseq, nblk),
        in_specs=[
            pl.BlockSpec(memory_space=pltpu.SMEM),
            pl.BlockSpec(bias.shape, lambda b, i: (0, 0)),
            pl.BlockSpec((BLOCK, ATT_HEADS * hd), lambda b, i: (blk0 + b * nblk + i, 0)),
            near(-1, kcol), near(0, kcol), near(1, kcol),
            near(-1, kcol + 1), near(0, kcol + 1), near(1, kcol + 1),
            cache_spec, cache_spec,
        ],
        out_specs=pl.BlockSpec((BLOCK, ATT_HEADS * hd), lambda b, i: (b * nblk + i, 0)),
        out_shape=jax.ShapeDtypeStruct((nseq * n, ATT_HEADS * hd), F32),
        compiler_params=_params("parallel", "parallel"),
        name="attn_latent",
    )(sink, bias, qkv, qkv, qkv, qkv, qkv, qkv, qkv, cache_k, cache_v)


def _rope_tables(n, pad):
    quarter = LANES // 4
    inv = ROPE_BASE ** (-jnp.arange(quarter, dtype=F32) / quarter)
    pos = jnp.arange(n)
    row_pos = (pos // GRID_W).astype(F32)
    col_pos = (pos % GRID_W).astype(F32)
    ang_r = row_pos[:, None] * inv[None, :]
    ang_c = col_pos[:, None] * inv[None, :]
    zero = jnp.zeros_like(ang_r)
    cos = jnp.concatenate([jnp.cos(ang_r), jnp.cos(ang_r), jnp.cos(ang_c), jnp.cos(ang_c)], axis=1)
    sin_up = jnp.concatenate([-jnp.sin(ang_r), zero, -jnp.sin(ang_c), zero], axis=1)
    sin_dn = jnp.concatenate([zero, jnp.sin(ang_r), zero, jnp.sin(ang_c)], axis=1)
    ones, zeros = jnp.ones((pad, LANES), F32), jnp.zeros((pad, LANES), F32)
    return (jnp.concatenate([cos, ones]), jnp.concatenate([sin_up, zeros]), jnp.concatenate([sin_dn, zeros]))


def _lower_bounds(lb_param):
    p = jax.nn.softmax(lb_param.astype(F32), axis=0)
    cs = jnp.cumsum(p, axis=0)
    return cs - cs[:1]


def kernel(x_prompt, x_sample, cache_k, cache_v, state_hgrn, c, c_ctx, w_ada, b_ada, norm1, norm2, norm_final,
           w_gate_up, w_down, w_in_a, lower_bounds, gnorm_a, w_out_a, w_qkv_b, w_out_b, sink_b):
    batch, seq, d = x_prompt.shape
    dec_batch, dec_seq, _ = x_sample.shape
    depth = w_ada.shape[0]
    t_ctx = batch * seq
    t_lat = dec_batch * dec_seq
    assert dec_batch + 1 <= SUBLANES and t_ctx % TOKEN_TILE == 0 and dec_seq % TOKEN_TILE == 0
    assert WINDOW == BLOCK

    def cond_row(tok0):
        return jnp.where(tok0 < t_ctx, 0, 1 + (tok0 - t_ctx) // dec_seq)

    x = (x_prompt.reshape(t_ctx, d), x_sample.reshape(t_lat, d))
    cond8 = jnp.concatenate([c_ctx[None, :], c, jnp.zeros((SUBLANES - 1 - dec_batch, d), F32)], axis=0)
    mod = _ada_call(cond8, w_ada, b_ada).reshape(depth, SUBLANES, 1, 6 * d)
    lb_all = _lower_bounds(lower_bounds)
    rope_tabs = _rope_tables(dec_seq, TOKEN_TILE)
    past = cache_k.shape[2]
    nf = norm_final.reshape(1, d)

    w_in_a, w_qkv_b, w_out_a, w_out_b, w_gate_up, w_down = (
        w.astype(BF16) for w in (w_in_a, w_qkv_b, w_out_a, w_out_b, w_gate_up, w_down))
    new_k = new_v = new_s = None
    for l in range(depth):
        j = l // 2
        n1 = norm1[l].reshape(1, d)
        n2 = norm2[l].reshape(1, d)
        if l % 2 == 0:
            proj, low = _proj_call(x, mod[l], n1, w_in_a, j, cond_row, HGRN_PROJ_TILE,
                                   lb_all[j].reshape(1, 2 * d))
            gn = gnorm_a[j].reshape(1, d)
            a_ctx, new_s = _gla_call(proj, low, gn, None, new_s, 0, batch, seq, GLA_HEADS_CTX, True,
                                     "hgrn_scan_ctx")
            (a_lat,) = _gla_call(proj, low, gn, state_hgrn[:, j], None, t_ctx, dec_batch, dec_seq, GLA_HEADS_LAT,
                                 False, "hgrn_scan_lat")
            w_mix = w_out_a
        else:
            qkv = _proj_call(x, mod[l], n1, w_qkv_b, j, cond_row, TOKEN_TILE, rope=(rope_tabs, dec_seq))
            sink = sink_b[j].reshape(KV_HEADS, GROUP)
            a_ctx, new_k, new_v = _attn_ctx_call(qkv, sink, new_k, new_v, batch, seq)
            a_lat = _attn_lat_call(qkv, cache_k[:, j].reshape(dec_batch, past, KV_HEADS * LANES),
                                   cache_v[:, j].reshape(dec_batch, past, KV_HEADS * LANES), sink,
                                   t_ctx, dec_batch, dec_seq)
            w_mix = w_out_b
        x = _ffn_call(x, (a_ctx, a_lat), mod[l], n2, w_mix, j, w_gate_up, w_down, l, nf, cond_row, TOKEN_TILE,
                      l == depth - 1)

    y_prompt = x[0].reshape(batch, seq, d)
    y_sample = x[1].reshape(dec_batch, dec_seq, d)
    return (y_prompt, y_sample, new_k, new_v, new_s)
```

```python
import functools

import jax
import jax.numpy as jnp
import numpy as np
from jax import lax
from jax.experimental import pallas as pl
from jax.experimental.pallas import tpu as pltpu

F32 = jnp.float32
BF16 = jnp.bfloat16
EPS = 1e-6
ROPE_BASE = 10000.0

V7X_VMEM_LIMIT_BYTES = 56 * 1024 * 1024
SUBLANES = 8
LANES = 128

A_HEADS = 8
ATT_HEADS = 8
KV_HEADS = 2
GROUP = ATT_HEADS // KV_HEADS
WINDOW = 128
BLOCK = 128
GRID_W = 64
TOKEN_TILE = 512
HGRN_PROJ_TILE = 256
GLA_CHUNK = 128
GLA_LEAF = SUBLANES
GLA_HEADS_CTX = 8
GLA_HEADS_LAT = 4
GLA_VMEM_HEADROOM_BYTES = 8 * 1024 * 1024
GLA_BLOCK = 32
GLA_RANGE_BOUND = 86.0
LOG2_E = 1.4426950408889634
NEG_BIG = -1e30


def _params(*sem):
    return pltpu.CompilerParams(dimension_semantics=sem, vmem_limit_bytes=V7X_VMEM_LIMIT_BYTES)


def _resident(shape):
    nd = len(shape)
    return pl.BlockSpec(shape, lambda *_: (0,) * nd, pipeline_mode=pl.Buffered(1))


def _layer(stacked, layer):
    return pl.BlockSpec((None,) + stacked.shape[1:], lambda *_: (layer, 0, 0), pipeline_mode=pl.Buffered(1))


def _div(x, k):
    assert k & (k - 1) == 0
    return jnp.right_shift(x, k.bit_length() - 1)


def _mod(x, k):
    assert k & (k - 1) == 0
    return jnp.bitwise_and(x, k - 1)


def _sigmoid(x):
    return 1.0 / (1.0 + jnp.exp(-x))


def _silu(x):
    return x * _sigmoid(x)


def _rmsnorm(x, gain):
    return x * lax.rsqrt(jnp.mean(x * x, axis=-1, keepdims=True) + EPS) * gain


def _dot(a, b):
    return jnp.dot(a, b, preferred_element_type=F32)


def _dot_nt(a, b):
    return lax.dot_general(a, b, (((1,), (1,)), ((), ())), preferred_element_type=F32)


def _ada_kernel(cond_ref, w_ref, b_ref, out_ref):
    s = _silu(cond_ref[...]).astype(BF16)
    out_ref[...] = _dot(s, w_ref[...].astype(BF16)) + b_ref[...]


def _ada_call(cond8, w_ada, b_ada):
    depth, d, n = w_ada.shape
    tn = n // 4
    return pl.pallas_call(
        _ada_kernel,
        grid=(depth, n // tn),
        in_specs=[
            pl.BlockSpec((SUBLANES, d), lambda l, j: (0, 0)),
            pl.BlockSpec((None, d, tn), lambda l, j: (l, 0, j)),
            pl.BlockSpec((None, 1, tn), lambda l, j: (l, 0, j)),
        ],
        out_specs=pl.BlockSpec((None, SUBLANES, tn), lambda l, j: (l, 0, j)),
        out_shape=jax.ShapeDtypeStruct((depth, SUBLANES, n), F32),
        compiler_params=_params("parallel", "parallel"),
        name="adaln_mod",
    )(cond8, w_ada, b_ada.reshape(depth, 1, n))


def _split_specs(tm, width, tiles0):
    return [pl.BlockSpec((tm, width), lambda i: (jnp.minimum(i, tiles0 - 1), 0)),
            pl.BlockSpec((tm, width), lambda i: (jnp.maximum(i - tiles0, 0), 0))]


def _load_split(ref0, ref1, tiles0):
    return jnp.where(pl.program_id(0) < tiles0, ref0[...], ref1[...])


def _store_split(ref0, ref1, tiles0, value):
    @pl.when(pl.program_id(0) < tiles0)
    def _():
        ref0[...] = value

    @pl.when(pl.program_id(0) >= tiles0)
    def _():
        ref1[...] = value


def _modulated(x0_ref, x1_ref, mod_ref, gain_ref, tiles0, d):
    shift = mod_ref[:, 0:d]
    scale = mod_ref[:, d:2 * d]
    x = _load_split(x0_ref, x1_ref, tiles0)
    return (_rmsnorm(x, gain_ref[...]) * (1.0 + scale) + shift).astype(BF16)


def _qkv_proj_kernel(x0_ref, x1_ref, mod_ref, gain_ref, w_ref, cos_ref, sa_ref, sb_ref, out_ref, *, d, tiles0):
    p = _dot(_modulated(x0_ref, x1_ref, mod_ref, gain_ref, tiles0, d), w_ref[...])
    cos = cos_ref[...]
    sa = sa_ref[...]
    sb = sb_ref[...]
    for h in range(p.shape[1] // LANES):
        x = p[:, h * LANES:(h + 1) * LANES]
        if h < ATT_HEADS + KV_HEADS:
            x = x * cos + pltpu.roll(x, LANES - 32, 1) * sa + pltpu.roll(x, 32, 1) * sb
        out_ref[:, h * LANES:(h + 1) * LANES] = x


def _hgrn_proj_kernel(x0_ref, x1_ref, mod_ref, gain_ref, lb_ref, w_ref, out_ref, low_ref, *, d, tiles0):
    h = _modulated(x0_ref, x1_ref, mod_ref, gain_ref, tiles0, d)
    low = None
    for part in (0, 1, 2, 4, 3):
        cols = slice(part * d, (part + 1) * d)
        p = _dot(h, w_ref[:, cols])
        if part == 0:
            p = _silu(p)
        elif part in (1, 2):
            lb = lb_ref[:, (part - 1) * d:part * d]
            p = jnp.log2(lb + (1.0 - lb) * _sigmoid(p))
            blocks = jnp.sum(p.reshape(p.shape[0] // GLA_BLOCK, GLA_BLOCK, d), axis=1)
            part_low = jnp.min(blocks, axis=0, keepdims=True)
            low = part_low if low is None else jnp.minimum(low, part_low)
        out_ref[:, cols] = p
    low_ref[...] = low


def _proj_call(x, mod, gain, w, layer, cond_row, tm, lb=None, rope=None):
    x0, x1 = x
    d = x0.shape[1]
    t = x0.shape[0] + x1.shape[0]
    tiles0 = x0.shape[0] // tm
    if lb is not None:
        body, name = _hgrn_proj_kernel, "hgrn_in_proj"
        extra_specs, extra_args = [_resident((1, 2 * d)), _layer(w, layer)], [lb, w]
    else:
        tables, n = rope
        per_seq = n // tm
        body, name = _qkv_proj_kernel, "attn_qkv_proj"
        table = pl.BlockSpec((tm, LANES), lambda i: (jnp.where(i < tiles0, per_seq, (i - tiles0) % per_seq), 0))
        extra_specs, extra_args = [_layer(w, layer), table, table, table], [w, *tables]
    out_specs = [pl.BlockSpec((tm, w.shape[2]), lambda i: (i, 0))]
    out_shape = [jax.ShapeDtypeStruct((t, w.shape[2]), F32)]
    if lb is not None:
        out_specs.append(pl.BlockSpec((None, 1, d), lambda i: (i, 0, 0)))
        out_shape.append(jax.ShapeDtypeStruct((t // tm, 1, d), F32))
    outs = pl.pallas_call(
        functools.partial(body, d=d, tiles0=tiles0),
        grid=(t // tm,),
        in_specs=_split_specs(tm, d, tiles0) + [
            pl.BlockSpec((None, 1, mod.shape[-1]), lambda i: (cond_row(i * tm), 0, 0)),
            _resident((1, d)),
        ] + extra_specs,
        out_specs=out_specs,
        out_shape=out_shape,
        compiler_params=_params("parallel"),
        name=name,
    )(x0, x1, mod, gain, *extra_args)
    return outs if lb is not None else outs[0]


def _ffn_kernel(x0_ref, x1_ref, a0_ref, a1_ref, mod_ref, gain_ref, wo_ref, wgu_ref, wd_ref, nf_ref, *rest,
                d, d_ff, ff_chunk, final, tiles0, gated):
    out_refs = rest[1:] if gated else rest
    g1 = mod_ref[:, 2 * d:3 * d]
    sh2 = mod_ref[:, 3 * d:4 * d]
    sc2 = mod_ref[:, 4 * d:5 * d]
    g2 = mod_ref[:, 5 * d:6 * d]
    a = _load_split(a0_ref, a1_ref, tiles0)
    if gated:
        a = a * _silu(rest[0][...])
    x1 = _load_split(x0_ref, x1_ref, tiles0) + g1 * _dot(a.astype(BF16), wo_ref[...])
    h = (_rmsnorm(x1, gain_ref[...]) * (1.0 + sc2) + sh2).astype(BF16)
    y = None
    for j in range(d_ff // ff_chunk):
        lo = j * ff_chunk
        g = _dot(h, wgu_ref[:, lo:lo + ff_chunk])
        u = _dot(h, wgu_ref[:, d_ff + lo:d_ff + lo + ff_chunk])
        part = _dot((_silu(g) * u).astype(BF16), wd_ref[lo:lo + ff_chunk, :])
        y = part if y is None else y + part
    x2 = x1 + g2 * y
    if final:
        x2 = _rmsnorm(x2, nf_ref[...])
    _store_split(out_refs[0], out_refs[1], tiles0, x2)


def _ffn_call(x, a, mod, gain, w_out, mixer_layer, w_gu, w_d, layer, norm_final, cond_row, tm, final, gate=None):
    (x0, x1), (a0, a1) = x, a
    d = x0.shape[1]
    d_ff = w_d.shape[1]
    ff_chunk = d_ff
    tiles0 = x0.shape[0] // tm
    rows = _split_specs(tm, d, tiles0)
    gate_specs, gate_args = [], []
    if gate is not None:
        gate_array, gate_block = gate
        gate_specs, gate_args = [pl.BlockSpec((tm, d), lambda i: (i, gate_block))], [gate_array]
    return pl.pallas_call(
        functools.partial(_ffn_kernel, d=d, d_ff=d_ff, ff_chunk=ff_chunk, final=final, tiles0=tiles0,
                          gated=gate is not None),
        grid=((x0.shape[0] + x1.shape[0]) // tm,),
        in_specs=rows + rows + [
            pl.BlockSpec((None, 1, mod.shape[-1]), lambda i: (cond_row(i * tm), 0, 0)),
            _resident((1, d)),
            _layer(w_out, mixer_layer),
            _layer(w_gu, layer),
            _layer(w_d, layer),
            _resident((1, d)),
        ] + gate_specs,
        out_specs=rows,
        out_shape=[jax.ShapeDtypeStruct(x0.shape, F32), jax.ShapeDtypeStruct(x1.shape, F32)],
        compiler_params=_params("arbitrary"),
        name="outproj_ffn",
    )(x0, x1, a0, a1, mod, gain, w_out, w_gu, w_d, norm_final, *gate_args)


def _row_blocks(b, rows, size):
    parts = [jnp.broadcast_to(b[r:r + 1, :], (size, b.shape[1])) for r in rows]
    return parts[0] if len(parts) == 1 else jnp.concatenate(parts, axis=0)


def _scores_blockwise(q, k, b, reverse):
    c, dk = q.shape
    blk = GLA_BLOCK
    nblk = c // blk
    zero_row = jnp.zeros((1, dk), F32)
    rows = []
    for i in range(nblk):
        lo, hi = i * blk, (i + 1) * blk
        if reverse:
            r = b[hi:hi + 1, :] if i < nblk - 1 else zero_row
            key_lo, key_hi = lo, c
        else:
            r = b[lo - 1:lo, :] if i > 0 else zero_row
            key_lo, key_hi = 0, hi
        qi = (q[lo:hi] * jnp.exp2(b[lo:hi] - r)).astype(BF16)
        ki = (k[key_lo:key_hi] * jnp.exp2(r - b[key_lo:key_hi])).astype(BF16)
        if key_hi - key_lo < c:
            pad = jnp.zeros((c - (key_hi - key_lo), dk), BF16)
            ki = jnp.concatenate([pad, ki] if reverse else [ki, pad], axis=0)
        rows.append(_dot_nt(qi, ki))
    return jnp.concatenate(rows, axis=0)


def _causal(c, reverse):
    t_idx = lax.broadcasted_iota(jnp.int32, (c, c), 0)
    s_idx = lax.broadcasted_iota(jnp.int32, (c, c), 1)
    return (s_idx >= t_idx) if reverse else (s_idx <= t_idx)


def _gla_decay_sums(lg, tri):
    hi = lg.astype(BF16)
    mid = (lg - hi.astype(F32)).astype(BF16)
    return _dot(jnp.where(tri, 1.0, 0.0).astype(BF16), jnp.concatenate([hi, mid], axis=1))


def _gla_mix(q, lg, b2, v, st, tri, reverse, bounded):
    c = q.shape[0]
    dk = q.shape[1]
    k = 1.0 - jnp.exp2(lg)
    b = b2[:, 0:dk] + b2[:, dk:2 * dk]
    b_end = b[0:1, :] if reverse else b[c - 1:c, :]
    o = _dot_nt((q * jnp.exp2(b)).astype(BF16), st.astype(BF16))
    k_dec = (k * jnp.exp2(b_end - b)).astype(BF16)
    st_new = st * jnp.exp2(b_end) + _dot(v.T.astype(BF16), k_dec)
    if bounded:
        scores = jnp.where(tri, _scores_blockwise(q, k, b, reverse), 0.0)
    else:
        scores = _scores_robust(q, k, b, tri, reverse)
    return o, st_new, scores.astype(BF16)


def _scores_robust(q, k, b, tri, reverse):
    c, dk = q.shape
    t_idx = lax.broadcasted_iota(jnp.int32, (c, c), 0)
    s_idx = lax.broadcasted_iota(jnp.int32, (c, c), 1)
    pos = lax.broadcasted_iota(jnp.int32, (c, dk), 0)
    scores = jnp.zeros((c, c), F32)
    m = c // 2
    while m >= GLA_LEAF:
        nblk = c // (2 * m)
        bound = [kb * 2 * m + (m if reverse else m - 1) for kb in range(nblk)]
        e = jnp.exp2(-jnp.abs(b - _row_blocks(b, bound, 2 * m)))
        late = _mod(pos, 2 * m) >= m
        q_side = jnp.logical_not(late) if reverse else late
        ql = jnp.where(q_side, q * e, 0.0).astype(BF16)
        kl = jnp.where(q_side, 0.0, k * e).astype(BF16)
        s_l = _dot_nt(ql, kl)
        if nblk > 1:
            s_l = jnp.where(_div(t_idx, 2 * m) == _div(s_idx, 2 * m), s_l, 0.0)
        scores = scores + s_l
        m //= 2
    nleaf = c // GLA_LEAF
    k3 = k.reshape(nleaf, GLA_LEAF, dk)
    b3 = b.reshape(nleaf, GLA_LEAF, dk)
    sel_d = _mod(lax.broadcasted_iota(jnp.int32, (dk, c), 1), GLA_LEAF)
    r = jnp.zeros((c, c), F32)
    for sl in range(GLA_LEAF):
        ks = jnp.broadcast_to(k3[:, sl:sl + 1, :], k3.shape).reshape(c, dk)
        bs = jnp.broadcast_to(b3[:, sl:sl + 1, :], b3.shape).reshape(c, dk)
        term = (q * ks * jnp.exp2(jnp.minimum(b - bs, 0.0))).astype(BF16)
        r = r + _dot(term, jnp.where(sel_d == sl, 1.0, 0.0).astype(BF16))
    same_leaf = _div(t_idx, GLA_LEAF) == _div(s_idx, GLA_LEAF)
    return scores + jnp.where(jnp.logical_and(same_leaf, tri), r, 0.0)


def _gla_kernel(*refs, n, hb, has_s0, nprev, want_state):
    q_ref, lgf_ref, lgb_ref, v_ref, gn_ref, low_ref = refs[:6]
    pos = 6
    s0_ref = prev_ref = sfin_ref = None
    if has_s0:
        s0_ref = refs[pos]
        pos += 1
    if nprev:
        prev_ref = refs[pos]
        pos += 1
    o_ref = refs[pos]
    pos += 1
    if want_state:
        sfin_ref = refs[pos]
        pos += 1
    half_ref, st_ref = refs[pos:pos + 2]
    c = GLA_CHUNK
    hd = LANES
    nchunks = n // c

    for direction in (0, 1):
        for h in range(hb):
            if has_s0:
                st_ref[direction, h] = s0_ref[direction, h].T
            else:
                st_ref[direction, h] = jnp.zeros((hd, hd), F32)

    bounded = jnp.min(low_ref[...]) >= -GLA_RANGE_BOUND

    def scan(is_bounded):
        def body(i, carry, closing):
            tri = (_causal(c, False), _causal(c, True))
            chains = []
            for h in range(hb):
                cols = slice(h * hd, (h + 1) * hd)
                for direction in (0, 1):
                    ci = i if direction == 0 else nchunks - 1 - i
                    rows = pl.ds(pl.multiple_of(ci * c, c), c)
                    lg = (lgf_ref if direction == 0 else lgb_ref)[rows, cols]
                    chains.append((h, direction, rows, cols, lg, _gla_decay_sums(lg, tri[direction])))
            mixed = []
            for h, direction, rows, cols, lg, b2 in chains:
                mixed.append(_gla_mix(q_ref[rows, cols], lg, b2, v_ref[rows, cols], st_ref[direction, h],
                                      tri[direction], direction == 1, is_bounded))
            for (h, direction, rows, cols, _, _), (o, st_new, scores) in zip(chains, mixed):
                st_ref[direction, h] = st_new
                o = o + _dot(scores, v_ref[rows, cols].astype(BF16))
                if closing:
                    o = o + half_ref[rows, cols]
                    o = o * lax.rsqrt(jnp.mean(o * o, axis=-1, keepdims=True) + EPS) * gn_ref[:, cols]
                    o_ref[rows, cols] = o
                else:
                    half_ref[rows, cols] = o
            return carry

        lax.fori_loop(0, nchunks // 2, functools.partial(body, closing=False), 0)
        lax.fori_loop(nchunks // 2, nchunks, functools.partial(body, closing=True), 0)

    pl.when(bounded)(functools.partial(scan, True))
    pl.when(jnp.logical_not(bounded))(functools.partial(scan, False))

    if want_state:
        if nprev:
            sfin_ref[0:nprev] = prev_ref[...]
        for direction in (0, 1):
            for h in range(hb):
                sfin_ref[nprev, direction, h] = st_ref[direction, h].T


def _gla_call(proj, low, gnorm, s0, prev, row0, nseq, n, hb, want_state, name):
    hd = LANES
    heads = A_HEADS
    hblocks = heads // hb
    assert n % (2 * GLA_CHUNK) == 0
    blk0 = row0 // n

    assert (2 * 5 + 1) * n * hb * hd * 4 <= V7X_VMEM_LIMIT_BYTES - GLA_VMEM_HEADROOM_BYTES

    def col(kind):
        return pl.BlockSpec((n, hb * hd), lambda b, h: (blk0 + b, kind * hblocks + h))

    in_specs = [col(0), col(1), col(2), col(3),
                pl.BlockSpec((1, hb * hd), lambda b, h: (0, h)),
                pl.BlockSpec((n // HGRN_PROJ_TILE, 1, hb * hd), lambda b, h: (blk0 + b, 0, h))]
    args = [proj, proj, proj, proj, gnorm, low]
    state_spec = pl.BlockSpec((None, 2, hb, hd, hd), lambda b, h: (b, 0, h, 0, 0))
    if s0 is not None:
        in_specs.append(state_spec)
        args.append(s0)
    nprev = 0 if prev is None else prev.shape[1]

    def layered(layers):
        return pl.BlockSpec((None, layers, 2, hb, hd, hd), lambda b, h: (b, 0, 0, h, 0, 0))

    if nprev:
        in_specs.append(layered(nprev))
        args.append(prev)
    out_shape = [jax.ShapeDtypeStruct((nseq * n, heads * hd), F32)]
    out_specs = [pl.BlockSpec((n, hb * hd), lambda b, h: (b, h))]
    if want_state:
        out_shape.append(jax.ShapeDtypeStruct((nseq, nprev + 1, 2, heads, hd, hd), F32))
        out_specs.append(layered(nprev + 1))
    return pl.pallas_call(
        functools.partial(_gla_kernel, n=n, hb=hb, has_s0=s0 is not None, nprev=nprev, want_state=want_state),
        grid=(nseq, hblocks),
        in_specs=in_specs,
        out_specs=out_specs,
        out_shape=out_shape,
        scratch_shapes=[pltpu.VMEM((n, hb * hd), F32), pltpu.VMEM((2, hb, hd, hd), F32)],
        compiler_params=_params("parallel", "parallel"),
        name=name,
    )(*args)


def _stacked_queries(q_ref, kh, scale):
    heads = [q_ref[:, (kh * GROUP + g) * LANES:(kh * GROUP + g + 1) * LANES] for g in range(GROUP)]
    return (jnp.concatenate(heads, axis=0) * scale).astype(BF16)


def _sink_column(sink_ref, kh, rows):
    gi = _div(lax.broadcasted_iota(jnp.int32, (GROUP * rows, 1), 0), rows)
    col = jnp.zeros((GROUP * rows, 1), F32)
    for g in range(GROUP):
        col = jnp.where(gi == g, sink_ref[kh, g] * LOG2_E, col)
    return col


def _softmax_pv(logits, sinks, values):
    mx = [jnp.maximum(jnp.max(lg, axis=-1, keepdims=True), s) for lg, s in zip(logits, sinks)]
    p = [jnp.exp2(lg - m) for lg, m in zip(logits, mx)]
    den = [jnp.sum(pi, axis=-1, keepdims=True) + jnp.exp2(s - m) for pi, s, m in zip(p, sinks, mx)]
    return [_dot(pi.astype(BF16), v) * (1.0 / d) for pi, v, d in zip(p, values, den)]


def _store_heads(o_ref, kh, o):
    rows = o_ref.shape[0]
    for g in range(GROUP):
        o_ref[:, (kh * GROUP + g) * LANES:(kh * GROUP + g + 1) * LANES] = o[g * rows:(g + 1) * rows, :]


def _attn_ctx_kernel(sink_ref, q_ref, k_ref, v_ref, *rest, scale, nprev):
    if nprev:
        pk_ref, pv_ref, o_ref, nk_ref, nv_ref = rest
        nk_ref[0:nprev] = pk_ref[...]
        nv_ref[0:nprev] = pv_ref[...]
    else:
        o_ref, nk_ref, nv_ref = rest
    for kh in range(KV_HEADS):
        nk_ref[nprev, :, kh, :] = k_ref[:, kh * LANES:(kh + 1) * LANES]
        nv_ref[nprev, :, kh, :] = v_ref[:, kh * LANES:(kh + 1) * LANES]
    rows = q_ref.shape[0]
    heads = range(KV_HEADS)
    q = [_stacked_queries(q_ref, kh, scale) for kh in heads]
    logits = [_dot_nt(q[kh], k_ref[:, kh * LANES:(kh + 1) * LANES].astype(BF16)) for kh in heads]
    sinks = [_sink_column(sink_ref, kh, rows) for kh in heads]
    values = [v_ref[:, kh * LANES:(kh + 1) * LANES].astype(BF16) for kh in heads]
    for kh, o in zip(heads, _softmax_pv(logits, sinks, values)):
        _store_heads(o_ref, kh, o)


def _attn_ctx_call(qkv, sink, prev_k, prev_v, nseq, n):
    hd = LANES
    kvw = KV_HEADS * hd
    nprev = 0 if prev_k is None else prev_k.shape[1]

    def layered(layers):
        return pl.BlockSpec((None, layers, n, KV_HEADS, hd), lambda b: (b, 0, 0, 0, 0))

    kv_shape = jax.ShapeDtypeStruct((nseq, nprev + 1, n, KV_HEADS, hd), F32)
    prev_specs = [layered(nprev), layered(nprev)] if nprev else []
    prev_args = [prev_k, prev_v] if nprev else []
    return pl.pallas_call(
        functools.partial(_attn_ctx_kernel, scale=hd ** -0.5 * LOG2_E, nprev=nprev),
        grid=(nseq,),
        in_specs=[
            pl.BlockSpec(memory_space=pltpu.SMEM),
            pl.BlockSpec((n, ATT_HEADS * hd), lambda b: (b, 0)),
            pl.BlockSpec((n, kvw), lambda b: (b, ATT_HEADS // KV_HEADS)),
            pl.BlockSpec((n, kvw), lambda b: (b, ATT_HEADS // KV_HEADS + 1)),
        ] + prev_specs,
        out_specs=[pl.BlockSpec((n, ATT_HEADS * hd), lambda b: (b, 0)), layered(nprev + 1), layered(nprev + 1)],
        out_shape=[jax.ShapeDtypeStruct((nseq * n, ATT_HEADS * hd), F32), kv_shape, kv_shape],
        compiler_params=_params("parallel"),
        name="attn_context",
    )(sink, qkv, qkv, qkv, *prev_args)


def _attn_lat_kernel(sink_ref, bias_ref, q_ref, k0_ref, k1_ref, k2_ref, v0_ref, v1_ref, v2_ref, kc_ref, vc_ref,
                     o_ref, *, scale, nblk):
    nb = pl.program_id(1)
    rows = q_ref.shape[0]
    past = kc_ref.shape[0]
    heads = range(KV_HEADS)
    bias_before = bias_ref[:, 0:BLOCK] + jnp.where(nb == 0, NEG_BIG, 0.0)
    bias_after = bias_ref[:, BLOCK:] + jnp.where(nb == nblk - 1, NEG_BIG, 0.0)

    def rows_of(kh, refs):
        return jnp.concatenate([r[:, kh * LANES:(kh + 1) * LANES] for r in refs], axis=0).astype(BF16)

    q = [_stacked_queries(q_ref, kh, scale) for kh in heads]
    raw = [_dot_nt(q[kh], rows_of(kh, (kc_ref, k0_ref, k1_ref, k2_ref))) for kh in heads]
    logits = [jnp.concatenate([lg[:, 0:past], lg[:, past:past + BLOCK] + bias_before,
                               lg[:, past + BLOCK:past + 2 * BLOCK], lg[:, past + 2 * BLOCK:] + bias_after], axis=1)
              for lg in raw]
    sinks = [_sink_column(sink_ref, kh, rows) for kh in heads]
    values = [rows_of(kh, (vc_ref, v0_ref, v1_ref, v2_ref)) for kh in heads]
    for kh, o in zip(heads, _softmax_pv(logits, sinks, values)):
        _store_heads(o_ref, kh, o)


def _attn_lat_call(qkv, cache_k, cache_v, sink, row0, nseq, n):
    hd = LANES
    nblk = n // BLOCK
    past = cache_k.shape[1]
    kcol = ATT_HEADS // KV_HEADS
    blk0 = row0 // BLOCK

    def near(off, colblock):
        def index(b, i):
            return (blk0 + b * nblk + jnp.clip(i + off, 0, nblk - 1), colblock)
        return pl.BlockSpec((BLOCK, KV_HEADS * hd), index)

    t_in = np.arange(GROUP * BLOCK)[:, None] % BLOCK
    j_in = np.arange(BLOCK)[None, :]
    bias = jnp.asarray(np.concatenate([np.where(j_in >= t_in, 0.0, NEG_BIG), np.where(j_in <= t_in, 0.0, NEG_BIG)],
                                      axis=1).astype(np.float32))
    cache_spec = pl.BlockSpec((None, past, KV_HEADS * hd), lambda b, i: (b, 0, 0))
    return pl.pallas_call(
        functools.partial(_attn_lat_kernel, scale=hd ** -0.5 * LOG2_E, nblk=nblk),
        grid=(nseq, nblk),
        in_specs=[
            pl.BlockSpec(memory_space=pltpu.SMEM),
            pl.BlockSpec(bias.shape, lambda b, i: (0, 0)),
            pl.BlockSpec((BLOCK, ATT_HEADS * hd), lambda b, i: (blk0 + b * nblk + i, 0)),
            near(-1, kcol), near(0, kcol), near(1, kcol),
            near(-1, kcol + 1), near(0, kcol + 1), near(1, kcol + 1),
            cache_spec, cache_spec,
        ],
        out_specs=pl.BlockSpec((BLOCK, ATT_HEADS * hd), lambda b, i: (b * nblk + i, 0)),
        out_shape=jax.ShapeDtypeStruct((nseq * n, ATT_HEADS * hd), F32),
        compiler_params=_params("parallel", "parallel"),
        name="attn_latent",
    )(sink, bias, qkv, qkv, qkv, qkv, qkv, qkv, qkv, cache_k, cache_v)


def _rope_tables(n, pad):
    quarter = LANES // 4
    inv = ROPE_BASE ** (-np.arange(quarter, dtype=np.float64) / quarter)
    pos = np.arange(n)
    ang_r = (pos // GRID_W)[:, None] * inv[None, :]
    ang_c = (pos % GRID_W)[:, None] * inv[None, :]
    zero = np.zeros_like(ang_r)
    cos = np.concatenate([np.cos(ang_r), np.cos(ang_r), np.cos(ang_c), np.cos(ang_c)], axis=1)
    sin_up = np.concatenate([-np.sin(ang_r), zero, -np.sin(ang_c), zero], axis=1)
    sin_dn = np.concatenate([zero, np.sin(ang_r), zero, np.sin(ang_c)], axis=1)
    ones, zeros = np.ones((pad, LANES)), np.zeros((pad, LANES))
    return tuple(jnp.asarray(np.concatenate(t).astype(np.float32))
                 for t in ((cos, ones), (sin_up, zeros), (sin_dn, zeros)))


def _lower_bounds(lb_param):
    p = jax.nn.softmax(lb_param.astype(F32), axis=0)
    cs = jnp.cumsum(p, axis=0)
    return cs - cs[:1]


def kernel(x_prompt, x_sample, cache_k, cache_v, state_hgrn, c, c_ctx, w_ada, b_ada, norm1, norm2, norm_final,
           w_gate_up, w_down, w_in_a, lower_bounds, gnorm_a, w_out_a, w_qkv_b, w_out_b, sink_b):
    batch, seq, d = x_prompt.shape
    dec_batch, dec_seq, _ = x_sample.shape
    depth = w_ada.shape[0]
    t_ctx = batch * seq
    t_lat = dec_batch * dec_seq
    assert dec_batch + 1 <= SUBLANES and t_ctx % TOKEN_TILE == 0 and dec_seq % TOKEN_TILE == 0
    assert WINDOW == BLOCK

    def cond_row(tok0):
        return jnp.where(tok0 < t_ctx, 0, 1 + (tok0 - t_ctx) // dec_seq)

    x = (x_prompt.reshape(t_ctx, d), x_sample.reshape(t_lat, d))
    cond8 = jnp.concatenate([c_ctx[None, :], c, jnp.zeros((SUBLANES - 1 - dec_batch, d), F32)], axis=0)
    mod = _ada_call(cond8, w_ada, b_ada).reshape(depth, SUBLANES, 1, 6 * d)
    lb_all = _lower_bounds(lower_bounds)
    rope_tabs = _rope_tables(dec_seq, TOKEN_TILE)
    past = cache_k.shape[2]
    nf = norm_final.reshape(1, d)

    w_in_a, w_qkv_b, w_out_a, w_out_b, w_gate_up, w_down = (
        w.astype(BF16) for w in (w_in_a, w_qkv_b, w_out_a, w_out_b, w_gate_up, w_down))
    new_k = new_v = new_s = None
    for l in range(depth):
        j = l // 2
        n1 = norm1[l].reshape(1, d)
        n2 = norm2[l].reshape(1, d)
        if l % 2 == 0:
            proj, low = _proj_call(x, mod[l], n1, w_in_a, j, cond_row, HGRN_PROJ_TILE,
                                   lb_all[j].reshape(1, 2 * d))
            gn = gnorm_a[j].reshape(1, d)
            a_ctx, new_s = _gla_call(proj, low, gn, None, new_s, 0, batch, seq, GLA_HEADS_CTX, True,
                                     "hgrn_scan_ctx")
            (a_lat,) = _gla_call(proj, low, gn, state_hgrn[:, j], None, t_ctx, dec_batch, dec_seq, GLA_HEADS_LAT,
                                 False, "hgrn_scan_lat")
            w_mix, gate = w_out_a, (proj, 4)
        else:
            qkv = _proj_call(x, mod[l], n1, w_qkv_b, j, cond_row, TOKEN_TILE, rope=(rope_tabs, dec_seq))
            sink = sink_b[j].reshape(KV_HEADS, GROUP)
            a_ctx, new_k, new_v = _attn_ctx_call(qkv, sink, new_k, new_v, batch, seq)
            a_lat = _attn_lat_call(qkv, cache_k[:, j].reshape(dec_batch, past, KV_HEADS * LANES),
                                   cache_v[:, j].reshape(dec_batch, past, KV_HEADS * LANES), sink,
                                   t_ctx, dec_batch, dec_seq)
            w_mix, gate = w_out_b, None
        x = _ffn_call(x, (a_ctx, a_lat), mod[l], n2, w_mix, j, w_gate_up, w_down, l, nf, cond_row, TOKEN_TILE,
                      l == depth - 1, gate)

    y_prompt = x[0].reshape(batch, seq, d)
    y_sample = x[1].reshape(dec_batch, dec_seq, d)
    return (y_prompt, y_sample, new_k, new_v, new_s)
```

```python
import functools

import jax
import jax.numpy as jnp
import numpy as np
from jax import lax
from jax.experimental import pallas as pl
from jax.experimental.pallas import tpu as pltpu

F32 = jnp.float32
BF16 = jnp.bfloat16
EPS = 1e-6
ROPE_BASE = 10000.0

V7X_VMEM_LIMIT_BYTES = 56 * 1024 * 1024
SUBLANES = 8
LANES = 128

A_HEADS = 8
ATT_HEADS = 8
KV_HEADS = 2
GROUP = ATT_HEADS // KV_HEADS
WINDOW = 128
BLOCK = 128
GRID_W = 64
ROPE_SPAN = LANES // 4
ADA_COL_TILES = 4
TOKEN_TILE = 512
QKV_PROJ_TILE = 1024
HGRN_PROJ_TILE = 256
GLA_CHUNK = 128
GLA_LEAF = SUBLANES
GLA_HEADS_CTX = 8
GLA_HEADS_LAT = 4
GLA_VMEM_HEADROOM_BYTES = 8 * 1024 * 1024
GLA_BLOCK = 32
GLA_RANGE_BOUND = 86.0
LOG2_E = 1.4426950408889634
NEG_BIG = -1e30


def _params(*sem):
    return pltpu.CompilerParams(dimension_semantics=sem, vmem_limit_bytes=V7X_VMEM_LIMIT_BYTES)


def _resident(shape):
    nd = len(shape)
    return pl.BlockSpec(shape, lambda *_: (0,) * nd, pipeline_mode=pl.Buffered(1))


def _layer(stacked, layer):
    return pl.BlockSpec((None,) + stacked.shape[1:], lambda *_: (layer, 0, 0), pipeline_mode=pl.Buffered(1))


def _div(x, k):
    assert k & (k - 1) == 0
    return jnp.right_shift(x, k.bit_length() - 1)


def _mod(x, k):
    assert k & (k - 1) == 0
    return jnp.bitwise_and(x, k - 1)


def _sigmoid(x):
    return 1.0 / (1.0 + jnp.exp(-x))


def _silu(x):
    return x * _sigmoid(x)


def _rmsnorm(x, gain):
    return x * lax.rsqrt(jnp.mean(x * x, axis=-1, keepdims=True) + EPS) * gain


def _dot(a, b):
    return jnp.dot(a, b, preferred_element_type=F32)


def _dot_nt(a, b):
    return lax.dot_general(a, b, (((1,), (1,)), ((), ())), preferred_element_type=F32)


def _ada_kernel(cond_ref, w_ref, b_ref, out_ref):
    s = _silu(cond_ref[...]).astype(BF16)
    out_ref[...] = _dot(s, w_ref[...].astype(BF16)) + b_ref[...]


def _ada_call(cond8, w_ada, b_ada):
    depth, d, n = w_ada.shape
    tn = n // ADA_COL_TILES
    return pl.pallas_call(
        _ada_kernel,
        grid=(depth, n // tn),
        in_specs=[
            pl.BlockSpec((SUBLANES, d), lambda l, j: (0, 0)),
            pl.BlockSpec((None, d, tn), lambda l, j: (l, 0, j)),
            pl.BlockSpec((None, 1, tn), lambda l, j: (l, 0, j)),
        ],
        out_specs=pl.BlockSpec((None, SUBLANES, tn), lambda l, j: (l, 0, j)),
        out_shape=jax.ShapeDtypeStruct((depth, SUBLANES, n), F32),
        compiler_params=_params("parallel", "parallel"),
        name="adaln_mod",
    )(cond8, w_ada, b_ada.reshape(depth, 1, n))


def _split_specs(tm, width, tiles0):
    return [pl.BlockSpec((tm, width), lambda i: (jnp.minimum(i, tiles0 - 1), 0)),
            pl.BlockSpec((tm, width), lambda i: (jnp.maximum(i - tiles0, 0), 0))]


def _load_split(ref0, ref1, tiles0):
    return jnp.where(pl.program_id(0) < tiles0, ref0[...], ref1[...])


def _store_split(ref0, ref1, tiles0, value):
    @pl.when(pl.program_id(0) < tiles0)
    def _():
        ref0[...] = value

    @pl.when(pl.program_id(0) >= tiles0)
    def _():
        ref1[...] = value


def _modulated(x0_ref, x1_ref, mod_ref, gain_ref, tiles0, d):
    shift = mod_ref[:, 0:d]
    scale = mod_ref[:, d:2 * d]
    x = _load_split(x0_ref, x1_ref, tiles0)
    return (_rmsnorm(x, gain_ref[...]) * (1.0 + scale) + shift).astype(BF16)


def _qkv_proj_kernel(x0_ref, x1_ref, mod_ref, gain_ref, w_ref, cos_ref, sa_ref, sb_ref, out_ref, *, d, tiles0):
    p = _dot(_modulated(x0_ref, x1_ref, mod_ref, gain_ref, tiles0, d), w_ref[...])
    cos = cos_ref[...]
    sa = sa_ref[...]
    sb = sb_ref[...]
    for h in range(p.shape[1] // LANES):
        x = p[:, h * LANES:(h + 1) * LANES]
        if h < ATT_HEADS + KV_HEADS:
            x = x * cos + pltpu.roll(x, LANES - ROPE_SPAN, 1) * sa + pltpu.roll(x, ROPE_SPAN, 1) * sb
        out_ref[:, h * LANES:(h + 1) * LANES] = x


def _hgrn_proj_kernel(x0_ref, x1_ref, mod_ref, gain_ref, lb_ref, w_ref, out_ref, low_ref, *, d, tiles0):
    h = _modulated(x0_ref, x1_ref, mod_ref, gain_ref, tiles0, d)
    low = None
    for part in (0, 1, 2, 4, 3):
        cols = slice(part * d, (part + 1) * d)
        p = _dot(h, w_ref[:, cols])
        if part == 0:
            p = _silu(p)
        elif part in (1, 2):
            lb = lb_ref[:, (part - 1) * d:part * d]
            p = jnp.log2(lb + (1.0 - lb) * _sigmoid(p))
            blocks = jnp.sum(p.reshape(p.shape[0] // GLA_BLOCK, GLA_BLOCK, d), axis=1)
            part_low = jnp.min(blocks, axis=0, keepdims=True)
            low = part_low if low is None else jnp.minimum(low, part_low)
        out_ref[:, cols] = p
    low_ref[...] = low


def _proj_call(x, mod, gain, w, layer, cond_row, tm, lb=None, rope=None):
    x0, x1 = x
    d = x0.shape[1]
    t = x0.shape[0] + x1.shape[0]
    tiles0 = x0.shape[0] // tm
    if lb is not None:
        body, name = _hgrn_proj_kernel, "hgrn_in_proj"
        extra_specs, extra_args = [_resident((1, 2 * d)), _layer(w, layer)], [lb, w]
    else:
        tables, n = rope
        per_seq = n // tm
        body, name = _qkv_proj_kernel, "attn_qkv_proj"
        table = pl.BlockSpec((tm, LANES), lambda i: (jnp.where(i < tiles0, per_seq, (i - tiles0) % per_seq), 0))
        extra_specs, extra_args = [_layer(w, layer), table, table, table], [w, *tables]
    out_specs = [pl.BlockSpec((tm, w.shape[2]), lambda i: (i, 0))]
    out_shape = [jax.ShapeDtypeStruct((t, w.shape[2]), F32)]
    if lb is not None:
        out_specs.append(pl.BlockSpec((None, 1, d), lambda i: (i, 0, 0)))
        out_shape.append(jax.ShapeDtypeStruct((t // tm, 1, d), F32))
    outs = pl.pallas_call(
        functools.partial(body, d=d, tiles0=tiles0),
        grid=(t // tm,),
        in_specs=_split_specs(tm, d, tiles0) + [
            pl.BlockSpec((None, 1, mod.shape[-1]), lambda i: (cond_row(i * tm), 0, 0)),
            _resident((1, d)),
        ] + extra_specs,
        out_specs=out_specs,
        out_shape=out_shape,
        compiler_params=_params("parallel"),
        name=name,
    )(x0, x1, mod, gain, *extra_args)
    return outs if lb is not None else outs[0]


def _ffn_kernel(x0_ref, x1_ref, a0_ref, a1_ref, mod_ref, gain_ref, wo_ref, wgu_ref, wd_ref, nf_ref, *rest,
                d, d_ff, ff_chunk, final, tiles0, gated):
    out_refs = rest[1:] if gated else rest
    g1 = mod_ref[:, 2 * d:3 * d]
    sh2 = mod_ref[:, 3 * d:4 * d]
    sc2 = mod_ref[:, 4 * d:5 * d]
    g2 = mod_ref[:, 5 * d:6 * d]
    a = _load_split(a0_ref, a1_ref, tiles0)
    if gated:
        a = a * _silu(rest[0][...])
    x1 = _load_split(x0_ref, x1_ref, tiles0) + g1 * _dot(a.astype(BF16), wo_ref[...])
    h = (_rmsnorm(x1, gain_ref[...]) * (1.0 + sc2) + sh2).astype(BF16)
    y = None
    for j in range(d_ff // ff_chunk):
        lo = j * ff_chunk
        g = _dot(h, wgu_ref[:, lo:lo + ff_chunk])
        u = _dot(h, wgu_ref[:, d_ff + lo:d_ff + lo + ff_chunk])
        part = _dot((_silu(g) * u).astype(BF16), wd_ref[lo:lo + ff_chunk, :])
        y = part if y is None else y + part
    x2 = x1 + g2 * y
    if final:
        x2 = _rmsnorm(x2, nf_ref[...])
    _store_split(out_refs[0], out_refs[1], tiles0, x2)


def _ffn_call(x, a, mod, gain, w_out, mixer_layer, w_gu, w_d, layer, norm_final, cond_row, tm, final, gate=None):
    (x0, x1), (a0, a1) = x, a
    d = x0.shape[1]
    d_ff = w_d.shape[1]
    ff_chunk = d_ff
    tiles0 = x0.shape[0] // tm
    rows = _split_specs(tm, d, tiles0)
    gate_specs, gate_args = [], []
    if gate is not None:
        gate_array, gate_block = gate
        gate_specs, gate_args = [pl.BlockSpec((tm, d), lambda i: (i, gate_block))], [gate_array]
    return pl.pallas_call(
        functools.partial(_ffn_kernel, d=d, d_ff=d_ff, ff_chunk=ff_chunk, final=final, tiles0=tiles0,
                          gated=gate is not None),
        grid=((x0.shape[0] + x1.shape[0]) // tm,),
        in_specs=rows + rows + [
            pl.BlockSpec((None, 1, mod.shape[-1]), lambda i: (cond_row(i * tm), 0, 0)),
            _resident((1, d)),
            _layer(w_out, mixer_layer),
            _layer(w_gu, layer),
            _layer(w_d, layer),
            _resident((1, d)),
        ] + gate_specs,
        out_specs=rows,
        out_shape=[jax.ShapeDtypeStruct(x0.shape, F32), jax.ShapeDtypeStruct(x1.shape, F32)],
        compiler_params=_params("arbitrary"),
        name="outproj_ffn",
    )(x0, x1, a0, a1, mod, gain, w_out, w_gu, w_d, norm_final, *gate_args)


def _row_blocks(b, rows, size):
    parts = [jnp.broadcast_to(b[r:r + 1, :], (size, b.shape[1])) for r in rows]
    return parts[0] if len(parts) == 1 else jnp.concatenate(parts, axis=0)


def _scores_blockwise(q, k, b, reverse):
    c, dk = q.shape
    blk = GLA_BLOCK
    nblk = c // blk
    zero_row = jnp.zeros((1, dk), F32)
    rows = []
    for i in range(nblk):
        lo, hi = i * blk, (i + 1) * blk
        if reverse:
            r = b[hi:hi + 1, :] if i < nblk - 1 else zero_row
            key_lo, key_hi = lo, c
        else:
            r = b[lo - 1:lo, :] if i > 0 else zero_row
            key_lo, key_hi = 0, hi
        qi = (q[lo:hi] * jnp.exp2(b[lo:hi] - r)).astype(BF16)
        ki = (k[key_lo:key_hi] * jnp.exp2(r - b[key_lo:key_hi])).astype(BF16)
        if key_hi - key_lo < c:
            pad = jnp.zeros((c - (key_hi - key_lo), dk), BF16)
            ki = jnp.concatenate([pad, ki] if reverse else [ki, pad], axis=0)
        rows.append(_dot_nt(qi, ki))
    return jnp.concatenate(rows, axis=0)


def _causal(c, reverse):
    t_idx = lax.broadcasted_iota(jnp.int32, (c, c), 0)
    s_idx = lax.broadcasted_iota(jnp.int32, (c, c), 1)
    return (s_idx >= t_idx) if reverse else (s_idx <= t_idx)


def _gla_decay_sums(lg, tri):
    hi = lg.astype(BF16)
    mid = (lg - hi.astype(F32)).astype(BF16)
    return _dot(jnp.where(tri, 1.0, 0.0).astype(BF16), jnp.concatenate([hi, mid], axis=1))


def _gla_mix(q, lg, b2, v, st, tri, reverse, bounded):
    c = q.shape[0]
    dk = q.shape[1]
    k = 1.0 - jnp.exp2(lg)
    b = b2[:, 0:dk] + b2[:, dk:2 * dk]
    b_end = b[0:1, :] if reverse else b[c - 1:c, :]
    o = _dot_nt((q * jnp.exp2(b)).astype(BF16), st.astype(BF16))
    k_dec = (k * jnp.exp2(b_end - b)).astype(BF16)
    st_new = st * jnp.exp2(b_end) + _dot(v.T.astype(BF16), k_dec)
    if bounded:
        scores = jnp.where(tri, _scores_blockwise(q, k, b, reverse), 0.0)
    else:
        scores = _scores_robust(q, k, b, tri, reverse)
    return o, st_new, scores.astype(BF16)


def _scores_robust(q, k, b, tri, reverse):
    c, dk = q.shape
    t_idx = lax.broadcasted_iota(jnp.int32, (c, c), 0)
    s_idx = lax.broadcasted_iota(jnp.int32, (c, c), 1)
    pos = lax.broadcasted_iota(jnp.int32, (c, dk), 0)
    scores = jnp.zeros((c, c), F32)
    m = c // 2
    while m >= GLA_LEAF:
        nblk = c // (2 * m)
        bound = [kb * 2 * m + (m if reverse else m - 1) for kb in range(nblk)]
        e = jnp.exp2(-jnp.abs(b - _row_blocks(b, bound, 2 * m)))
        late = _mod(pos, 2 * m) >= m
        q_side = jnp.logical_not(late) if reverse else late
        ql = jnp.where(q_side, q * e, 0.0).astype(BF16)
        kl = jnp.where(q_side, 0.0, k * e).astype(BF16)
        s_l = _dot_nt(ql, kl)
        if nblk > 1:
            s_l = jnp.where(_div(t_idx, 2 * m) == _div(s_idx, 2 * m), s_l, 0.0)
        scores = scores + s_l
        m //= 2
    nleaf = c // GLA_LEAF
    k3 = k.reshape(nleaf, GLA_LEAF, dk)
    b3 = b.reshape(nleaf, GLA_LEAF, dk)
    sel_d = _mod(lax.broadcasted_iota(jnp.int32, (dk, c), 1), GLA_LEAF)
    r = jnp.zeros((c, c), F32)
    for sl in range(GLA_LEAF):
        ks = jnp.broadcast_to(k3[:, sl:sl + 1, :], k3.shape).reshape(c, dk)
        bs = jnp.broadcast_to(b3[:, sl:sl + 1, :], b3.shape).reshape(c, dk)
        term = (q * ks * jnp.exp2(jnp.minimum(b - bs, 0.0))).astype(BF16)
        r = r + _dot(term, jnp.where(sel_d == sl, 1.0, 0.0).astype(BF16))
    same_leaf = _div(t_idx, GLA_LEAF) == _div(s_idx, GLA_LEAF)
    return scores + jnp.where(jnp.logical_and(same_leaf, tri), r, 0.0)


def _gla_kernel(*refs, n, hb, has_s0, nprev, want_state):
    q_ref, lgf_ref, lgb_ref, v_ref, gn_ref, low_ref = refs[:6]
    pos = 6
    s0_ref = prev_ref = sfin_ref = None
    if has_s0:
        s0_ref = refs[pos]
        pos += 1
    if nprev:
        prev_ref = refs[pos]
        pos += 1
    o_ref = refs[pos]
    pos += 1
    if want_state:
        sfin_ref = refs[pos]
        pos += 1
    half_ref, st_ref = refs[pos:pos + 2]
    c = GLA_CHUNK
    hd = LANES
    nchunks = n // c

    for direction in (0, 1):
        for h in range(hb):
            if has_s0:
                st_ref[direction, h] = s0_ref[direction, h].T
            else:
                st_ref[direction, h] = jnp.zeros((hd, hd), F32)

    bounded = jnp.min(low_ref[...]) >= -GLA_RANGE_BOUND

    def scan(is_bounded):
        def body(i, carry, closing):
            tri = (_causal(c, False), _causal(c, True))
            chains = []
            for h in range(hb):
                cols = slice(h * hd, (h + 1) * hd)
                for direction in (0, 1):
                    ci = i if direction == 0 else nchunks - 1 - i
                    rows = pl.ds(pl.multiple_of(ci * c, c), c)
                    lg = (lgf_ref if direction == 0 else lgb_ref)[rows, cols]
                    chains.append((h, direction, rows, cols, lg, _gla_decay_sums(lg, tri[direction])))
            mixed = []
            for h, direction, rows, cols, lg, b2 in chains:
                mixed.append(_gla_mix(q_ref[rows, cols], lg, b2, v_ref[rows, cols], st_ref[direction, h],
                                      tri[direction], direction == 1, is_bounded))
            for (h, direction, rows, cols, _, _), (o, st_new, scores) in zip(chains, mixed):
                st_ref[direction, h] = st_new
                o = o + _dot(scores, v_ref[rows, cols].astype(BF16))
                if closing:
                    o = o + half_ref[rows, cols]
                    o = o * lax.rsqrt(jnp.mean(o * o, axis=-1, keepdims=True) + EPS) * gn_ref[:, cols]
                    o_ref[rows, cols] = o
                else:
                    half_ref[rows, cols] = o
            return carry

        lax.fori_loop(0, nchunks // 2, functools.partial(body, closing=False), 0)
        lax.fori_loop(nchunks // 2, nchunks, functools.partial(body, closing=True), 0)

    pl.when(bounded)(functools.partial(scan, True))
    pl.when(jnp.logical_not(bounded))(functools.partial(scan, False))

    if want_state:
        if nprev:
            sfin_ref[0:nprev] = prev_ref[...]
        for direction in (0, 1):
            for h in range(hb):
                sfin_ref[nprev, direction, h] = st_ref[direction, h].T


def _gla_call(proj, low, gnorm, s0, prev, row0, nseq, n, hb, want_state, name):
    hd = LANES
    heads = A_HEADS
    hblocks = heads // hb
    assert n % (2 * GLA_CHUNK) == 0
    blk0 = row0 // n

    assert (2 * 5 + 1) * n * hb * hd * 4 <= V7X_VMEM_LIMIT_BYTES - GLA_VMEM_HEADROOM_BYTES

    def col(kind):
        return pl.BlockSpec((n, hb * hd), lambda b, h: (blk0 + b, kind * hblocks + h))

    in_specs = [col(0), col(1), col(2), col(3),
                pl.BlockSpec((1, hb * hd), lambda b, h: (0, h)),
                pl.BlockSpec((n // HGRN_PROJ_TILE, 1, hb * hd), lambda b, h: (blk0 + b, 0, h))]
    args = [proj, proj, proj, proj, gnorm, low]
    state_spec = pl.BlockSpec((None, 2, hb, hd, hd), lambda b, h: (b, 0, h, 0, 0))
    if s0 is not None:
        in_specs.append(state_spec)
        args.append(s0)
    nprev = 0 if prev is None else prev.shape[1]

    def layered(layers):
        return pl.BlockSpec((None, layers, 2, hb, hd, hd), lambda b, h: (b, 0, 0, h, 0, 0))

    if nprev:
        in_specs.append(layered(nprev))
        args.append(prev)
    out_shape = [jax.ShapeDtypeStruct((nseq * n, heads * hd), F32)]
    out_specs = [pl.BlockSpec((n, hb * hd), lambda b, h: (b, h))]
    if want_state:
        out_shape.append(jax.ShapeDtypeStruct((nseq, nprev + 1, 2, heads, hd, hd), F32))
        out_specs.append(layered(nprev + 1))
    return pl.pallas_call(
        functools.partial(_gla_kernel, n=n, hb=hb, has_s0=s0 is not None, nprev=nprev, want_state=want_state),
        grid=(nseq, hblocks),
        in_specs=in_specs,
        out_specs=out_specs,
        out_shape=out_shape,
        scratch_shapes=[pltpu.VMEM((n, hb * hd), F32), pltpu.VMEM((2, hb, hd, hd), F32)],
        compiler_params=_params("parallel", "parallel"),
        name=name,
    )(*args)


def _stacked_queries(q_ref, kh, scale):
    heads = [q_ref[:, (kh * GROUP + g) * LANES:(kh * GROUP + g + 1) * LANES] for g in range(GROUP)]
    return (jnp.concatenate(heads, axis=0) * scale).astype(BF16)


def _sink_column(sink_ref, kh, rows):
    gi = _div(lax.broadcasted_iota(jnp.int32, (GROUP * rows, 1), 0), rows)
    col = jnp.zeros((GROUP * rows, 1), F32)
    for g in range(GROUP):
        col = jnp.where(gi == g, sink_ref[kh, g] * LOG2_E, col)
    return col


def _softmax_pv(logits, sinks, values):
    mx = [jnp.maximum(jnp.max(lg, axis=-1, keepdims=True), s) for lg, s in zip(logits, sinks)]
    p = [jnp.exp2(lg - m) for lg, m in zip(logits, mx)]
    den = [jnp.sum(pi, axis=-1, keepdims=True) + jnp.exp2(s - m) for pi, s, m in zip(p, sinks, mx)]
    return [_dot(pi.astype(BF16), v) * (1.0 / d) for pi, v, d in zip(p, values, den)]


def _store_heads(o_ref, kh, o):
    rows = o_ref.shape[0]
    for g in range(GROUP):
        o_ref[:, (kh * GROUP + g) * LANES:(kh * GROUP + g + 1) * LANES] = o[g * rows:(g + 1) * rows, :]


def _attn_ctx_kernel(sink_ref, q_ref, k_ref, v_ref, *rest, scale, nprev):
    if nprev:
        pk_ref, pv_ref, o_ref, nk_ref, nv_ref = rest
        nk_ref[0:nprev] = pk_ref[...]
        nv_ref[0:nprev] = pv_ref[...]
    else:
        o_ref, nk_ref, nv_ref = rest
    for kh in range(KV_HEADS):
        nk_ref[nprev, :, kh, :] = k_ref[:, kh * LANES:(kh + 1) * LANES]
        nv_ref[nprev, :, kh, :] = v_ref[:, kh * LANES:(kh + 1) * LANES]
    rows = q_ref.shape[0]
    heads = range(KV_HEADS)
    q = [_stacked_queries(q_ref, kh, scale) for kh in heads]
    logits = [_dot_nt(q[kh], k_ref[:, kh * LANES:(kh + 1) * LANES].astype(BF16)) for kh in heads]
    sinks = [_sink_column(sink_ref, kh, rows) for kh in heads]
    values = [v_ref[:, kh * LANES:(kh + 1) * LANES].astype(BF16) for kh in heads]
    for kh, o in zip(heads, _softmax_pv(logits, sinks, values)):
        _store_heads(o_ref, kh, o)


def _attn_ctx_call(qkv, sink, prev_k, prev_v, nseq, n):
    hd = LANES
    kvw = KV_HEADS * hd
    nprev = 0 if prev_k is None else prev_k.shape[1]

    def layered(layers):
        return pl.BlockSpec((None, layers, n, KV_HEADS, hd), lambda b: (b, 0, 0, 0, 0))

    kv_shape = jax.ShapeDtypeStruct((nseq, nprev + 1, n, KV_HEADS, hd), F32)
    prev_specs = [layered(nprev), layered(nprev)] if nprev else []
    prev_args = [prev_k, prev_v] if nprev else []
    return pl.pallas_call(
        functools.partial(_attn_ctx_kernel, scale=hd ** -0.5 * LOG2_E, nprev=nprev),
        grid=(nseq,),
        in_specs=[
            pl.BlockSpec(memory_space=pltpu.SMEM),
            pl.BlockSpec((n, ATT_HEADS * hd), lambda b: (b, 0)),
            pl.BlockSpec((n, kvw), lambda b: (b, ATT_HEADS // KV_HEADS)),
            pl.BlockSpec((n, kvw), lambda b: (b, ATT_HEADS // KV_HEADS + 1)),
        ] + prev_specs,
        out_specs=[pl.BlockSpec((n, ATT_HEADS * hd), lambda b: (b, 0)), layered(nprev + 1), layered(nprev + 1)],
        out_shape=[jax.ShapeDtypeStruct((nseq * n, ATT_HEADS * hd), F32), kv_shape, kv_shape],
        compiler_params=_params("parallel"),
        name="attn_context",
    )(sink, qkv, qkv, qkv, *prev_args)


def _attn_lat_kernel(sink_ref, bias_ref, q_ref, k0_ref, k1_ref, k2_ref, v0_ref, v1_ref, v2_ref, kc_ref, vc_ref,
                     o_ref, *, scale, nblk):
    nb = pl.program_id(1)
    rows = q_ref.shape[0]
    past = kc_ref.shape[0]
    heads = range(KV_HEADS)
    bias_before = bias_ref[:, 0:BLOCK] + jnp.where(nb == 0, NEG_BIG, 0.0)
    bias_after = bias_ref[:, BLOCK:] + jnp.where(nb == nblk - 1, NEG_BIG, 0.0)

    def rows_of(kh, refs):
        return jnp.concatenate([r[:, kh * LANES:(kh + 1) * LANES] for r in refs], axis=0).astype(BF16)

    q = [_stacked_queries(q_ref, kh, scale) for kh in heads]
    raw = [_dot_nt(q[kh], rows_of(kh, (kc_ref, k0_ref, k1_ref, k2_ref))) for kh in heads]
    logits = [jnp.concatenate([lg[:, 0:past], lg[:, past:past + BLOCK] + bias_before,
                               lg[:, past + BLOCK:past + 2 * BLOCK], lg[:, past + 2 * BLOCK:] + bias_after], axis=1)
              for lg in raw]
    sinks = [_sink_column(sink_ref, kh, rows) for kh in heads]
    values = [rows_of(kh, (vc_ref, v0_ref, v1_ref, v2_ref)) for kh in heads]
    for kh, o in zip(heads, _softmax_pv(logits, sinks, values)):
        _store_heads(o_ref, kh, o)


def _attn_lat_call(qkv, cache_k, cache_v, sink, row0, nseq, n):
    hd = LANES
    nblk = n // BLOCK
    past = cache_k.shape[1]
    kcol = ATT_HEADS // KV_HEADS
    blk0 = row0 // BLOCK

    def near(off, colblock):
        def index(b, i):
            return (blk0 + b * nblk + jnp.clip(i + off, 0, nblk - 1), colblock)
        return pl.BlockSpec((BLOCK, KV_HEADS * hd), index)

    t_in = np.arange(GROUP * BLOCK)[:, None] % BLOCK
    j_in = np.arange(BLOCK)[None, :]
    bias = jnp.asarray(np.concatenate([np.where(j_in >= t_in, 0.0, NEG_BIG), np.where(j_in <= t_in, 0.0, NEG_BIG)],
                                      axis=1).astype(np.float32))
    cache_spec = pl.BlockSpec((None, past, KV_HEADS * hd), lambda b, i: (b, 0, 0))
    return pl.pallas_call(
        functools.partial(_attn_lat_kernel, scale=hd ** -0.5 * LOG2_E, nblk=nblk),
        grid=(nseq, nblk),
        in_specs=[
            pl.BlockSpec(memory_space=pltpu.SMEM),
            pl.BlockSpec(bias.shape, lambda b, i: (0, 0)),
            pl.BlockSpec((BLOCK, ATT_HEADS * hd), lambda b, i: (blk0 + b * nblk + i, 0)),
            near(-1, kcol), near(0, kcol), near(1, kcol),
            near(-1, kcol + 1), near(0, kcol + 1), near(1, kcol + 1),
            cache_spec, cache_spec,
        ],
        out_specs=pl.BlockSpec((BLOCK, ATT_HEADS * hd), lambda b, i: (b * nblk + i, 0)),
        out_shape=jax.ShapeDtypeStruct((nseq * n, ATT_HEADS * hd), F32),
        compiler_params=_params("parallel", "parallel"),
        name="attn_latent",
    )(sink, bias, qkv, qkv, qkv, qkv, qkv, qkv, qkv, cache_k, cache_v)


def _rope_tables(n, pad):
    quarter = ROPE_SPAN
    inv = ROPE_BASE ** (-np.arange(quarter, dtype=np.float64) / quarter)
    pos = np.arange(n)
    ang_r = (pos // GRID_W)[:, None] * inv[None, :]
    ang_c = (pos % GRID_W)[:, None] * inv[None, :]
    zero = np.zeros_like(ang_r)
    cos = np.concatenate([np.cos(ang_r), np.cos(ang_r), np.cos(ang_c), np.cos(ang_c)], axis=1)
    sin_up = np.concatenate([-np.sin(ang_r), zero, -np.sin(ang_c), zero], axis=1)
    sin_dn = np.concatenate([zero, np.sin(ang_r), zero, np.sin(ang_c)], axis=1)
    ones, zeros = np.ones((pad, LANES)), np.zeros((pad, LANES))
    return tuple(jnp.asarray(np.concatenate(t).astype(np.float32))
                 for t in ((cos, ones), (sin_up, zeros), (sin_dn, zeros)))


def _lower_bounds(lb_param):
    p = jax.nn.softmax(lb_param.astype(F32), axis=0)
    cs = jnp.cumsum(p, axis=0)
    return cs - cs[:1]


def kernel(x_prompt, x_sample, cache_k, cache_v, state_hgrn, c, c_ctx, w_ada, b_ada, norm1, norm2, norm_final,
           w_gate_up, w_down, w_in_a, lower_bounds, gnorm_a, w_out_a, w_qkv_b, w_out_b, sink_b):
    batch, seq, d = x_prompt.shape
    dec_batch, dec_seq, _ = x_sample.shape
    depth = w_ada.shape[0]
    t_ctx = batch * seq
    t_lat = dec_batch * dec_seq
    assert dec_batch + 1 <= SUBLANES
    assert all(t_ctx % tile == 0 and dec_seq % tile == 0 for tile in (TOKEN_TILE, QKV_PROJ_TILE, HGRN_PROJ_TILE))
    assert WINDOW == BLOCK

    def cond_row(tok0):
        return jnp.where(tok0 < t_ctx, 0, 1 + (tok0 - t_ctx) // dec_seq)

    x = (x_prompt.reshape(t_ctx, d), x_sample.reshape(t_lat, d))
    cond8 = jnp.concatenate([c_ctx[None, :], c, jnp.zeros((SUBLANES - 1 - dec_batch, d), F32)], axis=0)
    mod = _ada_call(cond8, w_ada, b_ada).reshape(depth, SUBLANES, 1, 6 * d)
    lb_all = _lower_bounds(lower_bounds)
    rope_tabs = _rope_tables(dec_seq, QKV_PROJ_TILE)
    past = cache_k.shape[2]
    nf = norm_final.reshape(1, d)

    w_in_a, w_qkv_b, w_out_a, w_out_b, w_gate_up, w_down = (
        w.astype(BF16) for w in (w_in_a, w_qkv_b, w_out_a, w_out_b, w_gate_up, w_down))
    new_k = new_v = new_s = None
    for l in range(depth):
        j = l // 2
        n1 = norm1[l].reshape(1, d)
        n2 = norm2[l].reshape(1, d)
        if l % 2 == 0:
            proj, low = _proj_call(x, mod[l], n1, w_in_a, j, cond_row, HGRN_PROJ_TILE,
                                   lb_all[j].reshape(1, 2 * d))
            gn = gnorm_a[j].reshape(1, d)
            a_ctx, new_s = _gla_call(proj, low, gn, None, new_s, 0, batch, seq, GLA_HEADS_CTX, True,
                                     "hgrn_scan_ctx")
            (a_lat,) = _gla_call(proj, low, gn, state_hgrn[:, j], None, t_ctx, dec_batch, dec_seq, GLA_HEADS_LAT,
                                 False, "hgrn_scan_lat")
            w_mix, gate = w_out_a, (proj, 4)
        else:
            qkv = _proj_call(x, mod[l], n1, w_qkv_b, j, cond_row, QKV_PROJ_TILE, rope=(rope_tabs, dec_seq))
            sink = sink_b[j].reshape(KV_HEADS, GROUP)
            a_ctx, new_k, new_v = _attn_ctx_call(qkv, sink, new_k, new_v, batch, seq)
            a_lat = _attn_lat_call(qkv, cache_k[:, j].reshape(dec_batch, past, KV_HEADS * LANES),
                                   cache_v[:, j].reshape(dec_batch, past, KV_HEADS * LANES), sink,
                                   t_ctx, dec_batch, dec_seq)
            w_mix, gate = w_out_b, None
        x = _ffn_call(x, (a_ctx, a_lat), mod[l], n2, w_mix, j, w_gate_up, w_down, l, nf, cond_row, TOKEN_TILE,
                      l == depth - 1, gate)

    y_prompt = x[0].reshape(batch, seq, d)
    y_sample = x[1].reshape(dec_batch, dec_seq, d)
    return (y_prompt, y_sample, new_k, new_v, new_s)
```

```python
import functools

import jax
import jax.numpy as jnp
import numpy as np
from jax import lax
from jax.experimental import pallas as pl
from jax.experimental.pallas import tpu as pltpu

F32 = jnp.float32
BF16 = jnp.bfloat16
EPS = 1e-6
ROPE_BASE = 10000.0

V7X_VMEM_LIMIT_BYTES = 56 * 1024 * 1024
SUBLANES = 8
LANES = 128

A_HEADS = 8
ATT_HEADS = 8
KV_HEADS = 2
GROUP = ATT_HEADS // KV_HEADS
WINDOW = 128
BLOCK = 128
GRID_W = 64
ROPE_SPAN = LANES // 4
ADA_COL_TILES = 4
TOKEN_TILE = 512
QKV_PROJ_TILE = 1024
HGRN_PROJ_TILE = 256
GLA_CHUNK = 128
GLA_LEAF = SUBLANES
GLA_HEADS_CTX = 8
GLA_HEADS_LAT = 4
GLA_VMEM_HEADROOM_BYTES = 8 * 1024 * 1024
GLA_BLOCK = 32
GLA_RANGE_BOUND = 86.0
LOG2_E = 1.4426950408889634
NEG_BIG = -1e30


def _params(*sem):
    return pltpu.CompilerParams(dimension_semantics=sem, vmem_limit_bytes=V7X_VMEM_LIMIT_BYTES)


def _resident(shape):
    nd = len(shape)
    return pl.BlockSpec(shape, lambda *_: (0,) * nd, pipeline_mode=pl.Buffered(1))


def _layer(stacked, layer):
    return pl.BlockSpec((None,) + stacked.shape[1:], lambda *_: (layer, 0, 0), pipeline_mode=pl.Buffered(1))


def _div(x, k):
    assert k & (k - 1) == 0
    return jnp.right_shift(x, k.bit_length() - 1)


def _mod(x, k):
    assert k & (k - 1) == 0
    return jnp.bitwise_and(x, k - 1)


def _sigmoid(x):
    return 1.0 / (1.0 + jnp.exp(-x))


def _silu(x):
    return x * _sigmoid(x)


def _rmsnorm(x, gain):
    return x * lax.rsqrt(jnp.mean(x * x, axis=-1, keepdims=True) + EPS) * gain


def _dot(a, b):
    return jnp.dot(a, b, preferred_element_type=F32)


def _dot_nt(a, b):
    return lax.dot_general(a, b, (((1,), (1,)), ((), ())), preferred_element_type=F32)


def _ada_kernel(cond_ref, w_ref, b_ref, out_ref):
    s = _silu(cond_ref[...]).astype(BF16)
    out_ref[...] = _dot(s, w_ref[...].astype(BF16)) + b_ref[...]


def _ada_call(cond8, w_ada, b_ada):
    depth, d, n = w_ada.shape
    tn = n // ADA_COL_TILES
    return pl.pallas_call(
        _ada_kernel,
        grid=(depth, n // tn),
        in_specs=[
            pl.BlockSpec((SUBLANES, d), lambda l, j: (0, 0)),
            pl.BlockSpec((None, d, tn), lambda l, j: (l, 0, j)),
            pl.BlockSpec((None, 1, tn), lambda l, j: (l, 0, j)),
        ],
        out_specs=pl.BlockSpec((None, SUBLANES, tn), lambda l, j: (l, 0, j)),
        out_shape=jax.ShapeDtypeStruct((depth, SUBLANES, n), F32),
        compiler_params=_params("parallel", "parallel"),
        name="adaln_mod",
    )(cond8, w_ada, b_ada.reshape(depth, 1, n))


def _split_specs(tm, width, tiles0):
    return [pl.BlockSpec((tm, width), lambda i: (jnp.minimum(i, tiles0 - 1), 0)),
            pl.BlockSpec((tm, width), lambda i: (jnp.maximum(i - tiles0, 0), 0))]


def _load_split(ref0, ref1, tiles0):
    return jnp.where(pl.program_id(0) < tiles0, ref0[...], ref1[...])


def _store_split(ref0, ref1, tiles0, value):
    @pl.when(pl.program_id(0) < tiles0)
    def _():
        ref0[...] = value

    @pl.when(pl.program_id(0) >= tiles0)
    def _():
        ref1[...] = value


def _modulated(x0_ref, x1_ref, mod_ref, gain_ref, tiles0, d):
    shift = mod_ref[:, 0:d]
    scale = mod_ref[:, d:2 * d]
    x = _load_split(x0_ref, x1_ref, tiles0)
    return (_rmsnorm(x, gain_ref[...]) * (1.0 + scale) + shift).astype(BF16)


def _qkv_proj_kernel(x0_ref, x1_ref, mod_ref, gain_ref, w_ref, cos_ref, sa_ref, sb_ref, out_ref, *, d, tiles0):
    p = _dot(_modulated(x0_ref, x1_ref, mod_ref, gain_ref, tiles0, d), w_ref[...].astype(BF16))
    cos = cos_ref[...]
    sa = sa_ref[...]
    sb = sb_ref[...]
    for h in range(p.shape[1] // LANES):
        x = p[:, h * LANES:(h + 1) * LANES]
        if h < ATT_HEADS + KV_HEADS:
            x = x * cos + pltpu.roll(x, LANES - ROPE_SPAN, 1) * sa + pltpu.roll(x, ROPE_SPAN, 1) * sb
        out_ref[:, h * LANES:(h + 1) * LANES] = x


def _hgrn_proj_kernel(x0_ref, x1_ref, mod_ref, gain_ref, lb_ref, w_ref, out_ref, low_ref, *, d, tiles0):
    h = _modulated(x0_ref, x1_ref, mod_ref, gain_ref, tiles0, d)
    low = None
    for part in (0, 1, 2, 4, 3):
        cols = slice(part * d, (part + 1) * d)
        p = _dot(h, w_ref[:, cols].astype(BF16))
        if part == 0:
            p = _silu(p)
        elif part in (1, 2):
            lb = lb_ref[:, (part - 1) * d:part * d]
            p = jnp.log2(lb + (1.0 - lb) * _sigmoid(p))
            blocks = jnp.sum(p.reshape(p.shape[0] // GLA_BLOCK, GLA_BLOCK, d), axis=1)
            part_low = jnp.min(blocks, axis=0, keepdims=True)
            low = part_low if low is None else jnp.minimum(low, part_low)
        out_ref[:, cols] = p
    low_ref[...] = low


def _proj_call(x, mod, gain, w, layer, cond_row, tm, lb=None, rope=None):
    x0, x1 = x
    d = x0.shape[1]
    t = x0.shape[0] + x1.shape[0]
    tiles0 = x0.shape[0] // tm
    if lb is not None:
        body, name = _hgrn_proj_kernel, "hgrn_in_proj"
        extra_specs, extra_args = [_resident((1, 2 * d)), _layer(w, layer)], [lb, w]
    else:
        tables, n = rope
        per_seq = n // tm
        body, name = _qkv_proj_kernel, "attn_qkv_proj"
        table = pl.BlockSpec((tm, LANES), lambda i: (jnp.where(i < tiles0, per_seq, (i - tiles0) % per_seq), 0))
        extra_specs, extra_args = [_layer(w, layer), table, table, table], [w, *tables]
    out_specs = [pl.BlockSpec((tm, w.shape[2]), lambda i: (i, 0))]
    out_shape = [jax.ShapeDtypeStruct((t, w.shape[2]), F32)]
    if lb is not None:
        out_specs.append(pl.BlockSpec((None, 1, d), lambda i: (i, 0, 0)))
        out_shape.append(jax.ShapeDtypeStruct((t // tm, 1, d), F32))
    outs = pl.pallas_call(
        functools.partial(body, d=d, tiles0=tiles0),
        grid=(t // tm,),
        in_specs=_split_specs(tm, d, tiles0) + [
            pl.BlockSpec((None, 1, mod.shape[-1]), lambda i: (cond_row(i * tm), 0, 0)),
            _resident((1, d)),
        ] + extra_specs,
        out_specs=out_specs,
        out_shape=out_shape,
        compiler_params=_params("parallel"),
        name=name,
    )(x0, x1, mod, gain, *extra_args)
    return outs if lb is not None else outs[0]


def _ffn_kernel(x0_ref, x1_ref, a0_ref, a1_ref, mod_ref, gain_ref, wo_ref, wgu_ref, wd_ref, nf_ref, *rest,
                d, d_ff, ff_chunk, final, tiles0, gated):
    out_refs = rest[1:] if gated else rest
    g1 = mod_ref[:, 2 * d:3 * d]
    sh2 = mod_ref[:, 3 * d:4 * d]
    sc2 = mod_ref[:, 4 * d:5 * d]
    g2 = mod_ref[:, 5 * d:6 * d]
    a = _load_split(a0_ref, a1_ref, tiles0)
    if gated:
        a = a * _silu(rest[0][...])
    x1 = _load_split(x0_ref, x1_ref, tiles0) + g1 * _dot(a.astype(BF16), wo_ref[...])
    h = (_rmsnorm(x1, gain_ref[...]) * (1.0 + sc2) + sh2).astype(BF16)
    y = None
    for j in range(d_ff // ff_chunk):
        lo = j * ff_chunk
        g = _dot(h, wgu_ref[:, lo:lo + ff_chunk])
        u = _dot(h, wgu_ref[:, d_ff + lo:d_ff + lo + ff_chunk])
        part = _dot((_silu(g) * u).astype(BF16), wd_ref[lo:lo + ff_chunk, :])
        y = part if y is None else y + part
    x2 = x1 + g2 * y
    if final:
        x2 = _rmsnorm(x2, nf_ref[...])
    _store_split(out_refs[0], out_refs[1], tiles0, x2)


def _ffn_call(x, a, mod, gain, w_out, mixer_layer, w_gu, w_d, layer, norm_final, cond_row, tm, final, gate=None):
    (x0, x1), (a0, a1) = x, a
    d = x0.shape[1]
    d_ff = w_d.shape[1]
    ff_chunk = d_ff
    tiles0 = x0.shape[0] // tm
    rows = _split_specs(tm, d, tiles0)
    gate_specs, gate_args = [], []
    if gate is not None:
        gate_array, gate_block = gate
        gate_specs, gate_args = [pl.BlockSpec((tm, d), lambda i: (i, gate_block))], [gate_array]
    return pl.pallas_call(
        functools.partial(_ffn_kernel, d=d, d_ff=d_ff, ff_chunk=ff_chunk, final=final, tiles0=tiles0,
                          gated=gate is not None),
        grid=((x0.shape[0] + x1.shape[0]) // tm,),
        in_specs=rows + rows + [
            pl.BlockSpec((None, 1, mod.shape[-1]), lambda i: (cond_row(i * tm), 0, 0)),
            _resident((1, d)),
            _layer(w_out, mixer_layer),
            _layer(w_gu, layer),
            _layer(w_d, layer),
            _resident((1, d)),
        ] + gate_specs,
        out_specs=rows,
        out_shape=[jax.ShapeDtypeStruct(x0.shape, F32), jax.ShapeDtypeStruct(x1.shape, F32)],
        compiler_params=_params("arbitrary"),
        name="outproj_ffn",
    )(x0, x1, a0, a1, mod, gain, w_out, w_gu, w_d, norm_final, *gate_args)


def _row_blocks(b, rows, size):
    parts = [jnp.broadcast_to(b[r:r + 1, :], (size, b.shape[1])) for r in rows]
    return parts[0] if len(parts) == 1 else jnp.concatenate(parts, axis=0)


def _scores_blockwise(q, k, b, reverse):
    c, dk = q.shape
    blk = GLA_BLOCK
    nblk = c // blk
    zero_row = jnp.zeros((1, dk), F32)
    rows = []
    for i in range(nblk):
        lo, hi = i * blk, (i + 1) * blk
        if reverse:
            r = b[hi:hi + 1, :] if i < nblk - 1 else zero_row
            key_lo, key_hi = lo, c
        else:
            r = b[lo - 1:lo, :] if i > 0 else zero_row
            key_lo, key_hi = 0, hi
        qi = (q[lo:hi] * jnp.exp2(b[lo:hi] - r)).astype(BF16)
        ki = (k[key_lo:key_hi] * jnp.exp2(r - b[key_lo:key_hi])).astype(BF16)
        if key_hi - key_lo < c:
            pad = jnp.zeros((c - (key_hi - key_lo), dk), BF16)
            ki = jnp.concatenate([pad, ki] if reverse else [ki, pad], axis=0)
        rows.append(_dot_nt(qi, ki))
    return jnp.concatenate(rows, axis=0)


def _causal(c, reverse):
    t_idx = lax.broadcasted_iota(jnp.int32, (c, c), 0)
    s_idx = lax.broadcasted_iota(jnp.int32, (c, c), 1)
    return (s_idx >= t_idx) if reverse else (s_idx <= t_idx)


def _gla_decay_sums(lg, ones):
    hi = lg.astype(BF16)
    mid = (lg - hi.astype(F32)).astype(BF16)
    return _dot(ones, jnp.concatenate([hi, mid], axis=1))


def _gla_mix(q, lg, b2, v, st, tri, reverse, bounded):
    c = q.shape[0]
    dk = q.shape[1]
    k = 1.0 - jnp.exp2(lg)
    b = b2[:, 0:dk] + b2[:, dk:2 * dk]
    b_end = b[0:1, :] if reverse else b[c - 1:c, :]
    o = _dot_nt((q * jnp.exp2(b)).astype(BF16), st.astype(BF16))
    k_dec = (k * jnp.exp2(b_end - b)).astype(BF16)
    st_new = st * jnp.exp2(b_end) + _dot(v.T.astype(BF16), k_dec)
    if bounded:
        scores = jnp.where(tri, _scores_blockwise(q, k, b, reverse), 0.0)
    else:
        scores = _scores_robust(q, k, b, tri, reverse)
    return o, st_new, scores.astype(BF16)


def _scores_robust(q, k, b, tri, reverse):
    c, dk = q.shape
    t_idx = lax.broadcasted_iota(jnp.int32, (c, c), 0)
    s_idx = lax.broadcasted_iota(jnp.int32, (c, c), 1)
    pos = lax.broadcasted_iota(jnp.int32, (c, dk), 0)
    scores = jnp.zeros((c, c), F32)
    m = c // 2
    while m >= GLA_LEAF:
        nblk = c // (2 * m)
        bound = [kb * 2 * m + (m if reverse else m - 1) for kb in range(nblk)]
        e = jnp.exp2(-jnp.abs(b - _row_blocks(b, bound, 2 * m)))
        late = _mod(pos, 2 * m) >= m
        q_side = jnp.logical_not(late) if reverse else late
        ql = jnp.where(q_side, q * e, 0.0).astype(BF16)
        kl = jnp.where(q_side, 0.0, k * e).astype(BF16)
        s_l = _dot_nt(ql, kl)
        if nblk > 1:
            s_l = jnp.where(_div(t_idx, 2 * m) == _div(s_idx, 2 * m), s_l, 0.0)
        scores = scores + s_l
        m //= 2
    nleaf = c // GLA_LEAF
    k3 = k.reshape(nleaf, GLA_LEAF, dk)
    b3 = b.reshape(nleaf, GLA_LEAF, dk)
    sel_d = _mod(lax.broadcasted_iota(jnp.int32, (dk, c), 1), GLA_LEAF)
    r = jnp.zeros((c, c), F32)
    for sl in range(GLA_LEAF):
        ks = jnp.broadcast_to(k3[:, sl:sl + 1, :], k3.shape).reshape(c, dk)
        bs = jnp.broadcast_to(b3[:, sl:sl + 1, :], b3.shape).reshape(c, dk)
        term = (q * ks * jnp.exp2(jnp.minimum(b - bs, 0.0))).astype(BF16)
        r = r + _dot(term, jnp.where(sel_d == sl, 1.0, 0.0).astype(BF16))
    same_leaf = _div(t_idx, GLA_LEAF) == _div(s_idx, GLA_LEAF)
    return scores + jnp.where(jnp.logical_and(same_leaf, tri), r, 0.0)


def _gla_kernel(*refs, n, hb, has_s0, nprev, want_state):
    q_ref, lgf_ref, lgb_ref, v_ref, gn_ref, low_ref = refs[:6]
    pos = 6
    s0_ref = prev_ref = sfin_ref = None
    if has_s0:
        s0_ref = refs[pos]
        pos += 1
    if nprev:
        prev_ref = refs[pos]
        pos += 1
    o_ref = refs[pos]
    pos += 1
    if want_state:
        sfin_ref = refs[pos]
        pos += 1
    half_ref, st_ref = refs[pos:pos + 2]
    c = GLA_CHUNK
    hd = LANES
    nchunks = n // c

    for direction in (0, 1):
        for h in range(hb):
            if has_s0:
                st_ref[direction, h] = s0_ref[direction, h].T
            else:
                st_ref[direction, h] = jnp.zeros((hd, hd), F32)

    bounded = jnp.min(low_ref[...]) >= -GLA_RANGE_BOUND

    def scan(is_bounded):
        def body(i, carry, closing):
            tri = (_causal(c, False), _causal(c, True))
            ones = tuple(jnp.where(t, 1.0, 0.0).astype(BF16) for t in tri)
            chains = []
            for h in range(hb):
                cols = slice(h * hd, (h + 1) * hd)
                for direction in (0, 1):
                    ci = i if direction == 0 else nchunks - 1 - i
                    rows = pl.ds(pl.multiple_of(ci * c, c), c)
                    lg = (lgf_ref if direction == 0 else lgb_ref)[rows, cols]
                    chains.append((h, direction, rows, cols, lg, _gla_decay_sums(lg, ones[direction])))
            mixed = []
            for h, direction, rows, cols, lg, b2 in chains:
                mixed.append(_gla_mix(q_ref[rows, cols], lg, b2, v_ref[rows, cols], st_ref[direction, h],
                                      tri[direction], direction == 1, is_bounded))
            for (h, direction, rows, cols, _, _), (o, st_new, scores) in zip(chains, mixed):
                st_ref[direction, h] = st_new
                o = o + _dot(scores, v_ref[rows, cols].astype(BF16))
                if closing:
                    o = o + half_ref[rows, cols]
                    o = o * lax.rsqrt(jnp.mean(o * o, axis=-1, keepdims=True) + EPS) * gn_ref[:, cols]
                    o_ref[rows, cols] = o
                else:
                    half_ref[rows, cols] = o
            return carry

        lax.fori_loop(0, nchunks // 2, functools.partial(body, closing=False), 0)
        lax.fori_loop(nchunks // 2, nchunks, functools.partial(body, closing=True), 0)

    pl.when(bounded)(functools.partial(scan, True))
    pl.when(jnp.logical_not(bounded))(functools.partial(scan, False))

    if want_state:
        if nprev:
            sfin_ref[0:nprev] = prev_ref[...]
        for direction in (0, 1):
            for h in range(hb):
                sfin_ref[nprev, direction, h] = st_ref[direction, h].T


def _gla_call(proj, low, gnorm, s0, prev, row0, nseq, n, hb, want_state, name):
    hd = LANES
    heads = A_HEADS
    hblocks = heads // hb
    assert n % (2 * GLA_CHUNK) == 0
    blk0 = row0 // n

    assert (2 * 5 + 1) * n * hb * hd * 4 <= V7X_VMEM_LIMIT_BYTES - GLA_VMEM_HEADROOM_BYTES

    def col(kind):
        return pl.BlockSpec((n, hb * hd), lambda b, h: (blk0 + b, kind * hblocks + h))

    in_specs = [col(0), col(1), col(2), col(3),
                pl.BlockSpec((1, hb * hd), lambda b, h: (0, h)),
                pl.BlockSpec((n // HGRN_PROJ_TILE, 1, hb * hd), lambda b, h: (blk0 + b, 0, h))]
    args = [proj, proj, proj, proj, gnorm, low]
    state_spec = pl.BlockSpec((None, 2, hb, hd, hd), lambda b, h: (b, 0, h, 0, 0))
    if s0 is not None:
        in_specs.append(state_spec)
        args.append(s0)
    nprev = 0 if prev is None else prev.shape[1]

    def layered(layers):
        return pl.BlockSpec((None, layers, 2, hb, hd, hd), lambda b, h: (b, 0, 0, h, 0, 0))

    if nprev:
        in_specs.append(layered(nprev))
        args.append(prev)
    out_shape = [jax.ShapeDtypeStruct((nseq * n, heads * hd), F32)]
    out_specs = [pl.BlockSpec((n, hb * hd), lambda b, h: (b, h))]
    if want_state:
        out_shape.append(jax.ShapeDtypeStruct((nseq, nprev + 1, 2, heads, hd, hd), F32))
        out_specs.append(layered(nprev + 1))
    return pl.pallas_call(
        functools.partial(_gla_kernel, n=n, hb=hb, has_s0=s0 is not None, nprev=nprev, want_state=want_state),
        grid=(nseq, hblocks),
        in_specs=in_specs,
        out_specs=out_specs,
        out_shape=out_shape,
        scratch_shapes=[pltpu.VMEM((n, hb * hd), F32), pltpu.VMEM((2, hb, hd, hd), F32)],
        compiler_params=_params("parallel", "parallel"),
        name=name,
    )(*args)


def _stacked_queries(q_ref, kh, scale):
    heads = [q_ref[:, (kh * GROUP + g) * LANES:(kh * GROUP + g + 1) * LANES] for g in range(GROUP)]
    return (jnp.concatenate(heads, axis=0) * scale).astype(BF16)


def _sink_column(sink_ref, kh, rows):
    gi = _div(lax.broadcasted_iota(jnp.int32, (GROUP * rows, 1), 0), rows)
    col = jnp.zeros((GROUP * rows, 1), F32)
    for g in range(GROUP):
        col = jnp.where(gi == g, sink_ref[kh, g] * LOG2_E, col)
    return col


def _softmax_pv(logits, sinks, values):
    mx = [jnp.maximum(jnp.max(lg, axis=-1, keepdims=True), s) for lg, s in zip(logits, sinks)]
    p = [jnp.exp2(lg - m) for lg, m in zip(logits, mx)]
    den = [jnp.sum(pi, axis=-1, keepdims=True) + jnp.exp2(s - m) for pi, s, m in zip(p, sinks, mx)]
    return [_dot(pi.astype(BF16), v) * (1.0 / d) for pi, v, d in zip(p, values, den)]


def _store_heads(o_ref, kh, o):
    rows = o_ref.shape[0]
    for g in range(GROUP):
        o_ref[:, (kh * GROUP + g) * LANES:(kh * GROUP + g + 1) * LANES] = o[g * rows:(g + 1) * rows, :]


def _attn_ctx_kernel(sink_ref, q_ref, k_ref, v_ref, *rest, scale, nprev):
    if nprev:
        pk_ref, pv_ref, o_ref, nk_ref, nv_ref = rest
        nk_ref[0:nprev] = pk_ref[...]
        nv_ref[0:nprev] = pv_ref[...]
    else:
        o_ref, nk_ref, nv_ref = rest
    for kh in range(KV_HEADS):
        nk_ref[nprev, :, kh, :] = k_ref[:, kh * LANES:(kh + 1) * LANES]
        nv_ref[nprev, :, kh, :] = v_ref[:, kh * LANES:(kh + 1) * LANES]
    rows = q_ref.shape[0]
    heads = range(KV_HEADS)
    q = [_stacked_queries(q_ref, kh, scale) for kh in heads]
    logits = [_dot_nt(q[kh], k_ref[:, kh * LANES:(kh + 1) * LANES].astype(BF16)) for kh in heads]
    sinks = [_sink_column(sink_ref, kh, rows) for kh in heads]
    values = [v_ref[:, kh * LANES:(kh + 1) * LANES].astype(BF16) for kh in heads]
    for kh, o in zip(heads, _softmax_pv(logits, sinks, values)):
        _store_heads(o_ref, kh, o)


def _attn_ctx_call(qkv, sink, prev_k, prev_v, nseq, n):
    hd = LANES
    kvw = KV_HEADS * hd
    nprev = 0 if prev_k is None else prev_k.shape[1]

    def layered(layers):
        return pl.BlockSpec((None, layers, n, KV_HEADS, hd), lambda b: (b, 0, 0, 0, 0))

    kv_shape = jax.ShapeDtypeStruct((nseq, nprev + 1, n, KV_HEADS, hd), F32)
    prev_specs = [layered(nprev), layered(nprev)] if nprev else []
    prev_args = [prev_k, prev_v] if nprev else []
    return pl.pallas_call(
        functools.partial(_attn_ctx_kernel, scale=hd ** -0.5 * LOG2_E, nprev=nprev),
        grid=(nseq,),
        in_specs=[
            pl.BlockSpec(memory_space=pltpu.SMEM),
            pl.BlockSpec((n, ATT_HEADS * hd), lambda b: (b, 0)),
            pl.BlockSpec((n, kvw), lambda b: (b, ATT_HEADS // KV_HEADS)),
            pl.BlockSpec((n, kvw), lambda b: (b, ATT_HEADS // KV_HEADS + 1)),
        ] + prev_specs,
        out_specs=[pl.BlockSpec((n, ATT_HEADS * hd), lambda b: (b, 0)), layered(nprev + 1), layered(nprev + 1)],
        out_shape=[jax.ShapeDtypeStruct((nseq * n, ATT_HEADS * hd), F32), kv_shape, kv_shape],
        compiler_params=_params("parallel"),
        name="attn_context",
    )(sink, qkv, qkv, qkv, *prev_args)


def _attn_lat_kernel(sink_ref, bias_ref, q_ref, k0_ref, k1_ref, k2_ref, v0_ref, v1_ref, v2_ref, kc_ref, vc_ref,
                     o_ref, *, scale, nblk):
    nb = pl.program_id(1)
    rows = q_ref.shape[0]
    past = kc_ref.shape[0]
    heads = range(KV_HEADS)
    bias_before = bias_ref[:, 0:BLOCK] + jnp.where(nb == 0, NEG_BIG, 0.0)
    bias_after = bias_ref[:, BLOCK:] + jnp.where(nb == nblk - 1, NEG_BIG, 0.0)

    def rows_of(kh, refs):
        return jnp.concatenate([r[:, kh * LANES:(kh + 1) * LANES] for r in refs], axis=0).astype(BF16)

    q = [_stacked_queries(q_ref, kh, scale) for kh in heads]
    raw = [_dot_nt(q[kh], rows_of(kh, (kc_ref, k0_ref, k1_ref, k2_ref))) for kh in heads]
    logits = [jnp.concatenate([lg[:, 0:past], lg[:, past:past + BLOCK] + bias_before,
                               lg[:, past + BLOCK:past + 2 * BLOCK], lg[:, past + 2 * BLOCK:] + bias_after], axis=1)
              for lg in raw]
    sinks = [_sink_column(sink_ref, kh, rows) for kh in heads]
    values = [rows_of(kh, (vc_ref, v0_ref, v1_ref, v2_ref)) for kh in heads]
    for kh, o in zip(heads, _softmax_pv(logits, sinks, values)):
        _store_heads(o_ref, kh, o)


def _attn_lat_call(qkv, cache_k, cache_v, sink, row0, nseq, n):
    hd = LANES
    nblk = n // BLOCK
    past = cache_k.shape[1]
    kcol = ATT_HEADS // KV_HEADS
    blk0 = row0 // BLOCK

    def near(off, colblock):
        def index(b, i):
            return (blk0 + b * nblk + jnp.clip(i + off, 0, nblk - 1), colblock)
        return pl.BlockSpec((BLOCK, KV_HEADS * hd), index)

    t_in = np.arange(GROUP * BLOCK)[:, None] % BLOCK
    j_in = np.arange(BLOCK)[None, :]
    bias = jnp.asarray(np.concatenate([np.where(j_in >= t_in, 0.0, NEG_BIG), np.where(j_in <= t_in, 0.0, NEG_BIG)],
                                      axis=1).astype(np.float32))
    cache_spec = pl.BlockSpec((None, past, KV_HEADS * hd), lambda b, i: (b, 0, 0))
    return pl.pallas_call(
        functools.partial(_attn_lat_kernel, scale=hd ** -0.5 * LOG2_E, nblk=nblk),
        grid=(nseq, nblk),
        in_specs=[
            pl.BlockSpec(memory_space=pltpu.SMEM),
            pl.BlockSpec(bias.shape, lambda b, i: (0, 0)),
            pl.BlockSpec((BLOCK, ATT_HEADS * hd), lambda b, i: (blk0 + b * nblk + i, 0)),
            near(-1, kcol), near(0, kcol), near(1, kcol),
            near(-1, kcol + 1), near(0, kcol + 1), near(1, kcol + 1),
            cache_spec, cache_spec,
        ],
        out_specs=pl.BlockSpec((BLOCK, ATT_HEADS * hd), lambda b, i: (b * nblk + i, 0)),
        out_shape=jax.ShapeDtypeStruct((nseq * n, ATT_HEADS * hd), F32),
        compiler_params=_params("parallel", "parallel"),
        name="attn_latent",
    )(sink, bias, qkv, qkv, qkv, qkv, qkv, qkv, qkv, cache_k, cache_v)


def _rope_tables(n, pad):
    quarter = ROPE_SPAN
    inv = ROPE_BASE ** (-np.arange(quarter, dtype=np.float64) / quarter)
    pos = np.arange(n)
    ang_r = (pos // GRID_W)[:, None] * inv[None, :]
    ang_c = (pos % GRID_W)[:, None] * inv[None, :]
    zero = np.zeros_like(ang_r)
    cos = np.concatenate([np.cos(ang_r), np.cos(ang_r), np.cos(ang_c), np.cos(ang_c)], axis=1)
    sin_up = np.concatenate([-np.sin(ang_r), zero, -np.sin(ang_c), zero], axis=1)
    sin_dn = np.concatenate([zero, np.sin(ang_r), zero, np.sin(ang_c)], axis=1)
    ones, zeros = np.ones((pad, LANES)), np.zeros((pad, LANES))
    return tuple(jnp.asarray(np.concatenate(t).astype(np.float32))
                 for t in ((cos, ones), (sin_up, zeros), (sin_dn, zeros)))


def _lower_bounds(lb_param):
    p = jax.nn.softmax(lb_param.astype(F32), axis=0)
    cs = jnp.cumsum(p, axis=0)
    return cs - cs[:1]


def kernel(x_prompt, x_sample, cache_k, cache_v, state_hgrn, c, c_ctx, w_ada, b_ada, norm1, norm2, norm_final,
           w_gate_up, w_down, w_in_a, lower_bounds, gnorm_a, w_out_a, w_qkv_b, w_out_b, sink_b):
    batch, seq, d = x_prompt.shape
    dec_batch, dec_seq, _ = x_sample.shape
    depth = w_ada.shape[0]
    t_ctx = batch * seq
    t_lat = dec_batch * dec_seq
    assert dec_batch + 1 <= SUBLANES
    assert all(t_ctx % tile == 0 and dec_seq % tile == 0 for tile in (TOKEN_TILE, QKV_PROJ_TILE, HGRN_PROJ_TILE))
    assert WINDOW == BLOCK

    def cond_row(tok0):
        return jnp.where(tok0 < t_ctx, 0, 1 + (tok0 - t_ctx) // dec_seq)

    x = (x_prompt.reshape(t_ctx, d), x_sample.reshape(t_lat, d))
    cond8 = jnp.concatenate([c_ctx[None, :], c, jnp.zeros((SUBLANES - 1 - dec_batch, d), F32)], axis=0)
    mod = _ada_call(cond8, w_ada, b_ada).reshape(depth, SUBLANES, 1, 6 * d)
    lb_all = _lower_bounds(lower_bounds)
    rope_tabs = _rope_tables(dec_seq, QKV_PROJ_TILE)
    past = cache_k.shape[2]
    nf = norm_final.reshape(1, d)

    w_out_a, w_out_b, w_gate_up, w_down = (w.astype(BF16) for w in (w_out_a, w_out_b, w_gate_up, w_down))
    new_k = new_v = new_s = None
    for l in range(depth):
        j = l // 2
        n1 = norm1[l].reshape(1, d)
        n2 = norm2[l].reshape(1, d)
        if l % 2 == 0:
            proj, low = _proj_call(x, mod[l], n1, w_in_a, j, cond_row, HGRN_PROJ_TILE,
                                   lb_all[j].reshape(1, 2 * d))
            gn = gnorm_a[j].reshape(1, d)
            a_ctx, new_s = _gla_call(proj, low, gn, None, new_s, 0, batch, seq, GLA_HEADS_CTX, True,
                                     "hgrn_scan_ctx")
            (a_lat,) = _gla_call(proj, low, gn, state_hgrn[:, j], None, t_ctx, dec_batch, dec_seq, GLA_HEADS_LAT,
                                 False, "hgrn_scan_lat")
            w_mix, gate = w_out_a, (proj, 4)
        else:
            qkv = _proj_call(x, mod[l], n1, w_qkv_b, j, cond_row, QKV_PROJ_TILE, rope=(rope_tabs, dec_seq))
            sink = sink_b[j].reshape(KV_HEADS, GROUP)
            a_ctx, new_k, new_v = _attn_ctx_call(qkv, sink, new_k, new_v, batch, seq)
            a_lat = _attn_lat_call(qkv, cache_k[:, j].reshape(dec_batch, past, KV_HEADS * LANES),
                                   cache_v[:, j].reshape(dec_batch, past, KV_HEADS * LANES), sink,
                                   t_ctx, dec_batch, dec_seq)
            w_mix, gate = w_out_b, None
        x = _ffn_call(x, (a_ctx, a_lat), mod[l], n2, w_mix, j, w_gate_up, w_down, l, nf, cond_row, TOKEN_TILE,
                      l == depth - 1, gate)

    y_prompt = x[0].reshape(batch, seq, d)
    y_sample = x[1].reshape(dec_batch, dec_seq, d)
    return (y_prompt, y_sample, new_k, new_v, new_s)
```

```python
import functools

import jax
import jax.numpy as jnp
import numpy as np
from jax import lax
from jax.experimental import pallas as pl
from jax.experimental.pallas import tpu as pltpu

F32 = jnp.float32
BF16 = jnp.bfloat16
EPS = 1e-6
ROPE_BASE = 10000.0

V7X_VMEM_LIMIT_BYTES = 56 * 1024 * 1024
SUBLANES = 8
LANES = 128

A_HEADS = 8
ATT_HEADS = 8
KV_HEADS = 2
GROUP = ATT_HEADS // KV_HEADS
WINDOW = 128
BLOCK = 128
GRID_W = 64
ROPE_SPAN = LANES // 4
ADA_COL_TILES = 4
TOKEN_TILE = 512
QKV_PROJ_TILE = 1024
HGRN_PROJ_TILE = 256
GLA_CHUNK = 128
GLA_LEAF = SUBLANES
GLA_HEADS_CTX = 8
GLA_HEADS_LAT = 4
GLA_VMEM_HEADROOM_BYTES = 8 * 1024 * 1024
GLA_BLOCK = 32
GLA_RANGE_BOUND = 86.0
LOG2_E = 1.4426950408889634
NEG_BIG = -1e30


def _params(*sem):
    return pltpu.CompilerParams(dimension_semantics=sem, vmem_limit_bytes=V7X_VMEM_LIMIT_BYTES)


def _resident(shape):
    nd = len(shape)
    return pl.BlockSpec(shape, lambda *_: (0,) * nd, pipeline_mode=pl.Buffered(1))


def _layer(stacked, layer):
    return pl.BlockSpec((None,) + stacked.shape[1:], lambda *_: (layer, 0, 0), pipeline_mode=pl.Buffered(1))


def _div(x, k):
    assert k & (k - 1) == 0
    return jnp.right_shift(x, k.bit_length() - 1)


def _mod(x, k):
    assert k & (k - 1) == 0
    return jnp.bitwise_and(x, k - 1)


def _sigmoid(x):
    return 1.0 / (1.0 + jnp.exp(-x))


def _silu(x):
    return x * _sigmoid(x)


def _rmsnorm(x, gain):
    return x * lax.rsqrt(jnp.mean(x * x, axis=-1, keepdims=True) + EPS) * gain


def _dot(a, b):
    return jnp.dot(a, b, preferred_element_type=F32)


def _dot_nt(a, b):
    return lax.dot_general(a, b, (((1,), (1,)), ((), ())), preferred_element_type=F32)


def _ada_kernel(cond_ref, w_ref, b_ref, out_ref):
    s = _silu(cond_ref[...]).astype(BF16)
    out_ref[...] = _dot(s, w_ref[...].astype(BF16)) + b_ref[...]


def _ada_call(cond8, w_ada, b_ada):
    depth, d, n = w_ada.shape
    tn = n // ADA_COL_TILES
    return pl.pallas_call(
        _ada_kernel,
        grid=(depth, n // tn),
        in_specs=[
            pl.BlockSpec((SUBLANES, d), lambda l, j: (0, 0)),
            pl.BlockSpec((None, d, tn), lambda l, j: (l, 0, j)),
            pl.BlockSpec((None, 1, tn), lambda l, j: (l, 0, j)),
        ],
        out_specs=pl.BlockSpec((None, SUBLANES, tn), lambda l, j: (l, 0, j)),
        out_shape=jax.ShapeDtypeStruct((depth, SUBLANES, n), F32),
        compiler_params=_params("parallel", "parallel"),
        name="adaln_mod",
    )(cond8, w_ada, b_ada.reshape(depth, 1, n))


def _split_specs(tm, width, tiles0):
    return [pl.BlockSpec((tm, width), lambda i: (jnp.minimum(i, tiles0 - 1), 0)),
            pl.BlockSpec((tm, width), lambda i: (jnp.maximum(i - tiles0, 0), 0))]


def _load_split(ref0, ref1, tiles0):
    return jnp.where(pl.program_id(0) < tiles0, ref0[...], ref1[...])


def _store_split(ref0, ref1, tiles0, value):
    @pl.when(pl.program_id(0) < tiles0)
    def _():
        ref0[...] = value

    @pl.when(pl.program_id(0) >= tiles0)
    def _():
        ref1[...] = value


def _modulated(x0_ref, x1_ref, mod_ref, gain_ref, tiles0, d):
    shift = mod_ref[:, 0:d]
    scale = mod_ref[:, d:2 * d]
    x = _load_split(x0_ref, x1_ref, tiles0)
    return (_rmsnorm(x, gain_ref[...]) * (1.0 + scale) + shift).astype(BF16)


def _qkv_proj_kernel(x0_ref, x1_ref, mod_ref, gain_ref, w_ref, cos_ref, sa_ref, sb_ref, out_ref, *, d, tiles0):
    p = _dot(_modulated(x0_ref, x1_ref, mod_ref, gain_ref, tiles0, d), w_ref[...].astype(BF16))
    cos = cos_ref[...]
    sa = sa_ref[...]
    sb = sb_ref[...]
    for h in range(p.shape[1] // LANES):
        x = p[:, h * LANES:(h + 1) * LANES]
        if h < ATT_HEADS + KV_HEADS:
            x = x * cos + pltpu.roll(x, LANES - ROPE_SPAN, 1) * sa + pltpu.roll(x, ROPE_SPAN, 1) * sb
        out_ref[:, h * LANES:(h + 1) * LANES] = x


def _hgrn_proj_kernel(x0_ref, x1_ref, mod_ref, gain_ref, lb_ref, w_ref, out_ref, low_ref, *, d, tiles0):
    h = _modulated(x0_ref, x1_ref, mod_ref, gain_ref, tiles0, d)
    low = None
    for part in (0, 1, 2, 4, 3):
        cols = slice(part * d, (part + 1) * d)
        p = _dot(h, w_ref[:, cols].astype(BF16))
        if part == 0:
            p = _silu(p)
        elif part in (1, 2):
            lb = lb_ref[:, (part - 1) * d:part * d]
            p = jnp.log2(lb + (1.0 - lb) * _sigmoid(p))
            blocks = jnp.sum(p.reshape(p.shape[0] // GLA_BLOCK, GLA_BLOCK, d), axis=1)
            part_low = jnp.min(blocks, axis=0, keepdims=True)
            low = part_low if low is None else jnp.minimum(low, part_low)
        out_ref[:, cols] = p
    low_ref[...] = low


def _proj_call(x, mod, gain, w, layer, cond_row, tm, lb=None, rope=None):
    x0, x1 = x
    d = x0.shape[1]
    t = x0.shape[0] + x1.shape[0]
    tiles0 = x0.shape[0] // tm
    if lb is not None:
        body, name = _hgrn_proj_kernel, "hgrn_in_proj"
        extra_specs, extra_args = [_resident((1, 2 * d)), _layer(w, layer)], [lb, w]
    else:
        tables, n = rope
        per_seq = n // tm
        body, name = _qkv_proj_kernel, "attn_qkv_proj"
        table = pl.BlockSpec((tm, LANES), lambda i: (jnp.where(i < tiles0, per_seq, (i - tiles0) % per_seq), 0))
        extra_specs, extra_args = [_layer(w, layer), table, table, table], [w, *tables]
    out_specs = [pl.BlockSpec((tm, w.shape[2]), lambda i: (i, 0))]
    out_shape = [jax.ShapeDtypeStruct((t, w.shape[2]), F32)]
    if lb is not None:
        out_specs.append(pl.BlockSpec((None, 1, d), lambda i: (i, 0, 0)))
        out_shape.append(jax.ShapeDtypeStruct((t // tm, 1, d), F32))
    outs = pl.pallas_call(
        functools.partial(body, d=d, tiles0=tiles0),
        grid=(t // tm,),
        in_specs=_split_specs(tm, d, tiles0) + [
            pl.BlockSpec((None, 1, mod.shape[-1]), lambda i: (cond_row(i * tm), 0, 0)),
            _resident((1, d)),
        ] + extra_specs,
        out_specs=out_specs,
        out_shape=out_shape,
        compiler_params=_params("parallel"),
        name=name,
    )(x0, x1, mod, gain, *extra_args)
    return outs if lb is not None else outs[0]


def _ffn_kernel(x0_ref, x1_ref, a0_ref, a1_ref, mod_ref, gain_ref, wo_ref, wgu_ref, wd_ref, nf_ref, *rest,
                d, d_ff, ff_chunk, final, tiles0, gated):
    out_refs = rest[1:] if gated else rest
    g1 = mod_ref[:, 2 * d:3 * d]
    sh2 = mod_ref[:, 3 * d:4 * d]
    sc2 = mod_ref[:, 4 * d:5 * d]
    g2 = mod_ref[:, 5 * d:6 * d]
    a = _load_split(a0_ref, a1_ref, tiles0)
    if gated:
        a = a * _silu(rest[0][...])
    x1 = _load_split(x0_ref, x1_ref, tiles0) + g1 * _dot(a.astype(BF16), wo_ref[...])
    h = (_rmsnorm(x1, gain_ref[...]) * (1.0 + sc2) + sh2).astype(BF16)
    y = None
    for j in range(d_ff // ff_chunk):
        lo = j * ff_chunk
        g = _dot(h, wgu_ref[:, lo:lo + ff_chunk])
        u = _dot(h, wgu_ref[:, d_ff + lo:d_ff + lo + ff_chunk])
        part = _dot((_silu(g) * u).astype(BF16), wd_ref[lo:lo + ff_chunk, :])
        y = part if y is None else y + part
    x2 = x1 + g2 * y
    if final:
        x2 = _rmsnorm(x2, nf_ref[...])
    _store_split(out_refs[0], out_refs[1], tiles0, x2)


def _ffn_call(x, a, mod, gain, w_out, mixer_layer, w_gu, w_d, layer, norm_final, cond_row, tm, final, gate=None):
    (x0, x1), (a0, a1) = x, a
    d = x0.shape[1]
    d_ff = w_d.shape[1]
    ff_chunk = d_ff
    tiles0 = x0.shape[0] // tm
    rows = _split_specs(tm, d, tiles0)
    gate_specs, gate_args = [], []
    if gate is not None:
        gate_array, gate_block = gate
        gate_specs, gate_args = [pl.BlockSpec((tm, d), lambda i: (i, gate_block))], [gate_array]
    return pl.pallas_call(
        functools.partial(_ffn_kernel, d=d, d_ff=d_ff, ff_chunk=ff_chunk, final=final, tiles0=tiles0,
                          gated=gate is not None),
        grid=((x0.shape[0] + x1.shape[0]) // tm,),
        in_specs=rows + rows + [
            pl.BlockSpec((None, 1, mod.shape[-1]), lambda i: (cond_row(i * tm), 0, 0)),
            _resident((1, d)),
            _layer(w_out, mixer_layer),
            _layer(w_gu, layer),
            _layer(w_d, layer),
            _resident((1, d)),
        ] + gate_specs,
        out_specs=rows,
        out_shape=[jax.ShapeDtypeStruct(x0.shape, F32), jax.ShapeDtypeStruct(x1.shape, F32)],
        compiler_params=_params("arbitrary"),
        name="outproj_ffn",
    )(x0, x1, a0, a1, mod, gain, w_out, w_gu, w_d, norm_final, *gate_args)


def _row_blocks(b, rows, size):
    parts = [jnp.broadcast_to(b[r:r + 1, :], (size, b.shape[1])) for r in rows]
    return parts[0] if len(parts) == 1 else jnp.concatenate(parts, axis=0)


def _scores_blockwise(q, k, b, reverse):
    c, dk = q.shape
    blk = GLA_BLOCK
    nblk = c // blk
    zero_row = jnp.zeros((1, dk), F32)
    rows = []
    for i in range(nblk):
        lo, hi = i * blk, (i + 1) * blk
        if reverse:
            r = b[hi:hi + 1, :] if i < nblk - 1 else zero_row
            key_lo, key_hi = lo, c
        else:
            r = b[lo - 1:lo, :] if i > 0 else zero_row
            key_lo, key_hi = 0, hi
        qi = (q[lo:hi] * jnp.exp2(b[lo:hi] - r)).astype(BF16)
        ki = (k[key_lo:key_hi] * jnp.exp2(r - b[key_lo:key_hi])).astype(BF16)
        if key_hi - key_lo < c:
            pad = jnp.zeros((c - (key_hi - key_lo), dk), BF16)
            ki = jnp.concatenate([pad, ki] if reverse else [ki, pad], axis=0)
        rows.append(_dot_nt(qi, ki))
    return jnp.concatenate(rows, axis=0)


def _causal(c, reverse):
    t_idx = lax.broadcasted_iota(jnp.int32, (c, c), 0)
    s_idx = lax.broadcasted_iota(jnp.int32, (c, c), 1)
    return (s_idx >= t_idx) if reverse else (s_idx <= t_idx)


def _gla_decay_sums(lg, ones):
    hi = lg.astype(BF16)
    mid = (lg - hi.astype(F32)).astype(BF16)
    return _dot(ones, jnp.concatenate([hi, mid], axis=1))


def _gla_mix(q, lg, b2, v, st, tri, reverse, bounded):
    c = q.shape[0]
    dk = q.shape[1]
    k = 1.0 - jnp.exp2(lg)
    b = b2[:, 0:dk] + b2[:, dk:2 * dk]
    b_end = b[0:1, :] if reverse else b[c - 1:c, :]
    o = _dot_nt((q * jnp.exp2(b)).astype(BF16), st.astype(BF16))
    k_dec = (k * jnp.exp2(b_end - b)).astype(BF16)
    st_new = st * jnp.exp2(b_end) + _dot(v.T.astype(BF16), k_dec)
    if bounded:
        scores = jnp.where(tri, _scores_blockwise(q, k, b, reverse), 0.0)
    else:
        scores = _scores_robust(q, k, b, tri, reverse)
    return o, st_new, scores.astype(BF16)


def _scores_robust(q, k, b, tri, reverse):
    c, dk = q.shape
    t_idx = lax.broadcasted_iota(jnp.int32, (c, c), 0)
    s_idx = lax.broadcasted_iota(jnp.int32, (c, c), 1)
    pos = lax.broadcasted_iota(jnp.int32, (c, dk), 0)
    scores = jnp.zeros((c, c), F32)
    m = c // 2
    while m >= GLA_LEAF:
        nblk = c // (2 * m)
        bound = [kb * 2 * m + (m if reverse else m - 1) for kb in range(nblk)]
        e = jnp.exp2(-jnp.abs(b - _row_blocks(b, bound, 2 * m)))
        late = _mod(pos, 2 * m) >= m
        q_side = jnp.logical_not(late) if reverse else late
        ql = jnp.where(q_side, q * e, 0.0).astype(BF16)
        kl = jnp.where(q_side, 0.0, k * e).astype(BF16)
        s_l = _dot_nt(ql, kl)
        if nblk > 1:
            s_l = jnp.where(_div(t_idx, 2 * m) == _div(s_idx, 2 * m), s_l, 0.0)
        scores = scores + s_l
        m //= 2
    nleaf = c // GLA_LEAF
    k3 = k.reshape(nleaf, GLA_LEAF, dk)
    b3 = b.reshape(nleaf, GLA_LEAF, dk)
    sel_d = _mod(lax.broadcasted_iota(jnp.int32, (dk, c), 1), GLA_LEAF)
    r = jnp.zeros((c, c), F32)
    for sl in range(GLA_LEAF):
        ks = jnp.broadcast_to(k3[:, sl:sl + 1, :], k3.shape).reshape(c, dk)
        bs = jnp.broadcast_to(b3[:, sl:sl + 1, :], b3.shape).reshape(c, dk)
        term = (q * ks * jnp.exp2(jnp.minimum(b - bs, 0.0))).astype(BF16)
        r = r + _dot(term, jnp.where(sel_d == sl, 1.0, 0.0).astype(BF16))
    same_leaf = _div(t_idx, GLA_LEAF) == _div(s_idx, GLA_LEAF)
    return scores + jnp.where(jnp.logical_and(same_leaf, tri), r, 0.0)


def _gla_kernel(*refs, n, hb, has_s0, nprev, want_state):
    q_ref, lgf_ref, lgb_ref, v_ref, gn_ref, low_ref = refs[:6]
    pos = 6
    s0_ref = prev_ref = sfin_ref = None
    if has_s0:
        s0_ref = refs[pos]
        pos += 1
    if nprev:
        prev_ref = refs[pos]
        pos += 1
    o_ref = refs[pos]
    pos += 1
    if want_state:
        sfin_ref = refs[pos]
        pos += 1
    half_ref, st_ref = refs[pos:pos + 2]
    c = GLA_CHUNK
    hd = LANES
    nchunks = n // c

    for direction in (0, 1):
        for h in range(hb):
            if has_s0:
                st_ref[direction, h] = s0_ref[direction, h].T
            else:
                st_ref[direction, h] = jnp.zeros((hd, hd), F32)

    bounded = jnp.min(low_ref[...]) >= -GLA_RANGE_BOUND

    def scan(is_bounded):
        def body(i, carry, closing):
            tri = (_causal(c, False), _causal(c, True))
            ones = tuple(jnp.where(t, 1.0, 0.0).astype(BF16) for t in tri)
            chains = []
            for h in range(hb):
                cols = slice(h * hd, (h + 1) * hd)
                for direction in (0, 1):
                    ci = i if direction == 0 else nchunks - 1 - i
                    rows = pl.ds(pl.multiple_of(ci * c, c), c)
                    lg = (lgf_ref if direction == 0 else lgb_ref)[rows, cols]
                    chains.append((h, direction, rows, cols, lg, _gla_decay_sums(lg, ones[direction])))
            mixed = []
            for h, direction, rows, cols, lg, b2 in chains:
                mixed.append(_gla_mix(q_ref[rows, cols], lg, b2, v_ref[rows, cols], st_ref[direction, h],
                                      tri[direction], direction == 1, is_bounded))
            for (h, direction, rows, cols, _, _), (o, st_new, scores) in zip(chains, mixed):
                st_ref[direction, h] = st_new
                o = o + _dot(scores, v_ref[rows, cols].astype(BF16))
                if closing:
                    o = o + half_ref[rows, cols]
                    o = o * lax.rsqrt(jnp.mean(o * o, axis=-1, keepdims=True) + EPS) * gn_ref[:, cols]
                    o_ref[rows, cols] = o
                else:
                    half_ref[rows, cols] = o
            return carry

        lax.fori_loop(0, nchunks // 2, functools.partial(body, closing=False), 0)
        lax.fori_loop(nchunks // 2, nchunks, functools.partial(body, closing=True), 0)
        if want_state:
            for direction in (0, 1):
                for h in range(hb):
                    sfin_ref[nprev, direction, h] = st_ref[direction, h].T

    if want_state and nprev:
        sfin_ref[0:nprev] = prev_ref[...]
    pl.when(bounded)(functools.partial(scan, True))
    pl.when(jnp.logical_not(bounded))(functools.partial(scan, False))


def _gla_call(proj, low, gnorm, s0, prev, row0, nseq, n, hb, want_state, name):
    hd = LANES
    heads = A_HEADS
    hblocks = heads // hb
    assert n % (2 * GLA_CHUNK) == 0
    blk0 = row0 // n

    assert (2 * 5 + 1) * n * hb * hd * 4 <= V7X_VMEM_LIMIT_BYTES - GLA_VMEM_HEADROOM_BYTES

    def col(kind):
        return pl.BlockSpec((n, hb * hd), lambda b, h: (blk0 + b, kind * hblocks + h))

    in_specs = [col(0), col(1), col(2), col(3),
                pl.BlockSpec((1, hb * hd), lambda b, h: (0, h)),
                pl.BlockSpec((n // HGRN_PROJ_TILE, 1, hb * hd), lambda b, h: (blk0 + b, 0, h))]
    args = [proj, proj, proj, proj, gnorm, low]
    state_spec = pl.BlockSpec((None, 2, hb, hd, hd), lambda b, h: (b, 0, h, 0, 0))
    if s0 is not None:
        in_specs.append(state_spec)
        args.append(s0)
    nprev = 0 if prev is None else prev.shape[1]

    def layered(layers):
        return pl.BlockSpec((None, layers, 2, hb, hd, hd), lambda b, h: (b, 0, 0, h, 0, 0))

    if nprev:
        in_specs.append(layered(nprev))
        args.append(prev)
    out_shape = [jax.ShapeDtypeStruct((nseq * n, heads * hd), F32)]
    out_specs = [pl.BlockSpec((n, hb * hd), lambda b, h: (b, h))]
    if want_state:
        out_shape.append(jax.ShapeDtypeStruct((nseq, nprev + 1, 2, heads, hd, hd), F32))
        out_specs.append(layered(nprev + 1))
    return pl.pallas_call(
        functools.partial(_gla_kernel, n=n, hb=hb, has_s0=s0 is not None, nprev=nprev, want_state=want_state),
        grid=(nseq, hblocks),
        in_specs=in_specs,
        out_specs=out_specs,
        out_shape=out_shape,
        scratch_shapes=[pltpu.VMEM((n, hb * hd), F32), pltpu.VMEM((2, hb, hd, hd), F32)],
        compiler_params=_params("parallel", "parallel"),
        name=name,
    )(*args)


def _stacked_queries(q_ref, kh, scale):
    heads = [q_ref[:, (kh * GROUP + g) * LANES:(kh * GROUP + g + 1) * LANES] for g in range(GROUP)]
    return (jnp.concatenate(heads, axis=0) * scale).astype(BF16)


def _sink_column(sink_ref, kh, rows):
    gi = _div(lax.broadcasted_iota(jnp.int32, (GROUP * rows, 1), 0), rows)
    col = jnp.zeros((GROUP * rows, 1), F32)
    for g in range(GROUP):
        col = jnp.where(gi == g, sink_ref[kh, g] * LOG2_E, col)
    return col


def _softmax_pv(logits, sinks, values):
    mx = [jnp.maximum(jnp.max(lg, axis=-1, keepdims=True), s) for lg, s in zip(logits, sinks)]
    p = [jnp.exp2(lg - m) for lg, m in zip(logits, mx)]
    den = [jnp.sum(pi, axis=-1, keepdims=True) + jnp.exp2(s - m) for pi, s, m in zip(p, sinks, mx)]
    return [_dot(pi.astype(BF16), v) * (1.0 / d) for pi, v, d in zip(p, values, den)]


def _store_heads(o_ref, kh, o):
    rows = o_ref.shape[0]
    for g in range(GROUP):
        o_ref[:, (kh * GROUP + g) * LANES:(kh * GROUP + g + 1) * LANES] = o[g * rows:(g + 1) * rows, :]


def _attn_ctx_kernel(sink_ref, q_ref, k_ref, v_ref, *rest, scale, nprev):
    if nprev:
        pk_ref, pv_ref, o_ref, nk_ref, nv_ref = rest
        nk_ref[0:nprev] = pk_ref[...]
        nv_ref[0:nprev] = pv_ref[...]
    else:
        o_ref, nk_ref, nv_ref = rest
    for kh in range(KV_HEADS):
        nk_ref[nprev, :, kh, :] = k_ref[:, kh * LANES:(kh + 1) * LANES]
        nv_ref[nprev, :, kh, :] = v_ref[:, kh * LANES:(kh + 1) * LANES]
    rows = q_ref.shape[0]
    heads = range(KV_HEADS)
    q = [_stacked_queries(q_ref, kh, scale) for kh in heads]
    logits = [_dot_nt(q[kh], k_ref[:, kh * LANES:(kh + 1) * LANES].astype(BF16)) for kh in heads]
    sinks = [_sink_column(sink_ref, kh, rows) for kh in heads]
    values = [v_ref[:, kh * LANES:(kh + 1) * LANES].astype(BF16) for kh in heads]
    for kh, o in zip(heads, _softmax_pv(logits, sinks, values)):
        _store_heads(o_ref, kh, o)


def _attn_ctx_call(qkv, sink, prev_k, prev_v, nseq, n):
    hd = LANES
    kvw = KV_HEADS * hd
    nprev = 0 if prev_k is None else prev_k.shape[1]

    def layered(layers):
        return pl.BlockSpec((None, layers, n, KV_HEADS, hd), lambda b: (b, 0, 0, 0, 0))

    kv_shape = jax.ShapeDtypeStruct((nseq, nprev + 1, n, KV_HEADS, hd), F32)
    prev_specs = [layered(nprev), layered(nprev)] if nprev else []
    prev_args = [prev_k, prev_v] if nprev else []
    return pl.pallas_call(
        functools.partial(_attn_ctx_kernel, scale=hd ** -0.5 * LOG2_E, nprev=nprev),
        grid=(nseq,),
        in_specs=[
            pl.BlockSpec(memory_space=pltpu.SMEM),
            pl.BlockSpec((n, ATT_HEADS * hd), lambda b: (b, 0)),
            pl.BlockSpec((n, kvw), lambda b: (b, ATT_HEADS // KV_HEADS)),
            pl.BlockSpec((n, kvw), lambda b: (b, ATT_HEADS // KV_HEADS + 1)),
        ] + prev_specs,
        out_specs=[pl.BlockSpec((n, ATT_HEADS * hd), lambda b: (b, 0)), layered(nprev + 1), layered(nprev + 1)],
        out_shape=[jax.ShapeDtypeStruct((nseq * n, ATT_HEADS * hd), F32), kv_shape, kv_shape],
        compiler_params=_params("parallel"),
        name="attn_context",
    )(sink, qkv, qkv, qkv, *prev_args)


def _attn_lat_kernel(sink_ref, bias_ref, q_ref, k0_ref, k1_ref, k2_ref, v0_ref, v1_ref, v2_ref, kc_ref, vc_ref,
                     o_ref, *, scale, nblk):
    nb = pl.program_id(1)
    rows = q_ref.shape[0]
    past = kc_ref.shape[0]
    heads = range(KV_HEADS)
    bias_before = bias_ref[:, 0:BLOCK] + jnp.where(nb == 0, NEG_BIG, 0.0)
    bias_after = bias_ref[:, BLOCK:] + jnp.where(nb == nblk - 1, NEG_BIG, 0.0)

    def rows_of(kh, refs):
        return jnp.concatenate([r[:, kh * LANES:(kh + 1) * LANES] for r in refs], axis=0).astype(BF16)

    q = [_stacked_queries(q_ref, kh, scale) for kh in heads]
    raw = [_dot_nt(q[kh], rows_of(kh, (kc_ref, k0_ref, k1_ref, k2_ref))) for kh in heads]
    logits = [jnp.concatenate([lg[:, 0:past], lg[:, past:past + BLOCK] + bias_before,
                               lg[:, past + BLOCK:past + 2 * BLOCK], lg[:, past + 2 * BLOCK:] + bias_after], axis=1)
              for lg in raw]
    sinks = [_sink_column(sink_ref, kh, rows) for kh in heads]
    values = [rows_of(kh, (vc_ref, v0_ref, v1_ref, v2_ref)) for kh in heads]
    for kh, o in zip(heads, _softmax_pv(logits, sinks, values)):
        _store_heads(o_ref, kh, o)


def _attn_lat_call(qkv, cache_k, cache_v, sink, row0, nseq, n):
    hd = LANES
    nblk = n // BLOCK
    past = cache_k.shape[1]
    kcol = ATT_HEADS // KV_HEADS
    blk0 = row0 // BLOCK

    def near(off, colblock):
        def index(b, i):
            return (blk0 + b * nblk + jnp.clip(i + off, 0, nblk - 1), colblock)
        return pl.BlockSpec((BLOCK, KV_HEADS * hd), index)

    t_in = np.arange(GROUP * BLOCK)[:, None] % BLOCK
    j_in = np.arange(BLOCK)[None, :]
    bias = jnp.asarray(np.concatenate([np.where(j_in >= t_in, 0.0, NEG_BIG), np.where(j_in <= t_in, 0.0, NEG_BIG)],
                                      axis=1).astype(np.float32))
    cache_spec = pl.BlockSpec((None, past, KV_HEADS * hd), lambda b, i: (b, 0, 0))
    return pl.pallas_call(
        functools.partial(_attn_lat_kernel, scale=hd ** -0.5 * LOG2_E, nblk=nblk),
        grid=(nseq, nblk),
        in_specs=[
            pl.BlockSpec(memory_space=pltpu.SMEM),
            pl.BlockSpec(bias.shape, lambda b, i: (0, 0)),
            pl.BlockSpec((BLOCK, ATT_HEADS * hd), lambda b, i: (blk0 + b * nblk + i, 0)),
            near(-1, kcol), near(0, kcol), near(1, kcol),
            near(-1, kcol + 1), near(0, kcol + 1), near(1, kcol + 1),
            cache_spec, cache_spec,
        ],
        out_specs=pl.BlockSpec((BLOCK, ATT_HEADS * hd), lambda b, i: (b * nblk + i, 0)),
        out_shape=jax.ShapeDtypeStruct((nseq * n, ATT_HEADS * hd), F32),
        compiler_params=_params("parallel", "parallel"),
        name="attn_latent",
    )(sink, bias, qkv, qkv, qkv, qkv, qkv, qkv, qkv, cache_k, cache_v)


def _rope_tables(n, pad):
    quarter = ROPE_SPAN
    inv = ROPE_BASE ** (-np.arange(quarter, dtype=np.float64) / quarter)
    pos = np.arange(n)
    ang_r = (pos // GRID_W)[:, None] * inv[None, :]
    ang_c = (pos % GRID_W)[:, None] * inv[None, :]
    zero = np.zeros_like(ang_r)
    cos = np.concatenate([np.cos(ang_r), np.cos(ang_r), np.cos(ang_c), np.cos(ang_c)], axis=1)
    sin_up = np.concatenate([-np.sin(ang_r), zero, -np.sin(ang_c), zero], axis=1)
    sin_dn = np.concatenate([zero, np.sin(ang_r), zero, np.sin(ang_c)], axis=1)
    ones, zeros = np.ones((pad, LANES)), np.zeros((pad, LANES))
    return tuple(jnp.asarray(np.concatenate(t).astype(np.float32))
                 for t in ((cos, ones), (sin_up, zeros), (sin_dn, zeros)))


def _lower_bounds(lb_param):
    p = jax.nn.softmax(lb_param.astype(F32), axis=0)
    cs = jnp.cumsum(p, axis=0)
    return cs - cs[:1]


def kernel(x_prompt, x_sample, cache_k, cache_v, state_hgrn, c, c_ctx, w_ada, b_ada, norm1, norm2, norm_final,
           w_gate_up, w_down, w_in_a, lower_bounds, gnorm_a, w_out_a, w_qkv_b, w_out_b, sink_b):
    batch, seq, d = x_prompt.shape
    dec_batch, dec_seq, _ = x_sample.shape
    depth = w_ada.shape[0]
    t_ctx = batch * seq
    t_lat = dec_batch * dec_seq
    assert dec_batch + 1 <= SUBLANES
    assert all(t_ctx % tile == 0 and dec_seq % tile == 0 for tile in (TOKEN_TILE, QKV_PROJ_TILE, HGRN_PROJ_TILE))
    assert WINDOW == BLOCK

    def cond_row(tok0):
        return jnp.where(tok0 < t_ctx, 0, 1 + (tok0 - t_ctx) // dec_seq)

    x = (x_prompt.reshape(t_ctx, d), x_sample.reshape(t_lat, d))
    cond8 = jnp.concatenate([c_ctx[None, :], c, jnp.zeros((SUBLANES - 1 - dec_batch, d), F32)], axis=0)
    mod = _ada_call(cond8, w_ada, b_ada).reshape(depth, SUBLANES, 1, 6 * d)
    lb_all = _lower_bounds(lower_bounds)
    rope_tabs = _rope_tables(dec_seq, QKV_PROJ_TILE)
    past = cache_k.shape[2]
    nf = norm_final.reshape(1, d)

    w_out_a, w_out_b, w_gate_up, w_down = (w.astype(BF16) for w in (w_out_a, w_out_b, w_gate_up, w_down))
    new_k = new_v = new_s = None
    for l in range(depth):
        j = l // 2
        n1 = norm1[l].reshape(1, d)
        n2 = norm2[l].reshape(1, d)
        if l % 2 == 0:
            proj, low = _proj_call(x, mod[l], n1, w_in_a, j, cond_row, HGRN_PROJ_TILE,
                                   lb_all[j].reshape(1, 2 * d))
            gn = gnorm_a[j].reshape(1, d)
            a_ctx, new_s = _gla_call(proj, low, gn, None, new_s, 0, batch, seq, GLA_HEADS_CTX, True,
                                     "hgrn_scan_ctx")
            (a_lat,) = _gla_call(proj, low, gn, state_hgrn[:, j], None, t_ctx, dec_batch, dec_seq, GLA_HEADS_LAT,
                                 False, "hgrn_scan_lat")
            w_mix, gate = w_out_a, (proj, 4)
        else:
            qkv = _proj_call(x, mod[l], n1, w_qkv_b, j, cond_row, QKV_PROJ_TILE, rope=(rope_tabs, dec_seq))
            sink = sink_b[j].reshape(KV_HEADS, GROUP)
            a_ctx, new_k, new_v = _attn_ctx_call(qkv, sink, new_k, new_v, batch, seq)
            a_lat = _attn_lat_call(qkv, cache_k[:, j].reshape(dec_batch, past, KV_HEADS * LANES),
                                   cache_v[:, j].reshape(dec_batch, past, KV_HEADS * LANES), sink,
                                   t_ctx, dec_batch, dec_seq)
            w_mix, gate = w_out_b, None
        x = _ffn_call(x, (a_ctx, a_lat), mod[l], n2, w_mix, j, w_gate_up, w_down, l, nf, cond_row, TOKEN_TILE,
                      l == depth - 1, gate)

    y_prompt = x[0].reshape(batch, seq, d)
    y_sample = x[1].reshape(dec_batch, dec_seq, d)
    return (y_prompt, y_sample, new_k, new_v, new_s)
```

```python
import functools

import jax
import jax.numpy as jnp
import numpy as np
from jax import lax
from jax.experimental import pallas as pl
from jax.experimental.pallas import tpu as pltpu

F32 = jnp.float32
BF16 = jnp.bfloat16
EPS = 1e-6
ROPE_BASE = 10000.0

V7X_VMEM_LIMIT_BYTES = 56 * 1024 * 1024
SUBLANES = 8
LANES = 128

A_HEADS = 8
ATT_HEADS = 8
KV_HEADS = 2
GROUP = ATT_HEADS // KV_HEADS
WINDOW = 128
BLOCK = 128
GRID_W = 64
ROPE_SPAN = LANES // 4
ADA_COL_TILES = 4
TOKEN_TILE = 512
QKV_PROJ_TILE = 1024
HGRN_PROJ_TILE = 256
GLA_CHUNK = 128
GLA_LEAF = SUBLANES
GLA_HEADS_CTX = 8
GLA_HEADS_LAT = 4
GLA_VMEM_HEADROOM_BYTES = 8 * 1024 * 1024
GLA_BLOCK = 32
GLA_RANGE_BOUND = 86.0
LOG2_E = 1.4426950408889634
NEG_BIG = -1e30


def _params(*sem):
    return pltpu.CompilerParams(dimension_semantics=sem, vmem_limit_bytes=V7X_VMEM_LIMIT_BYTES)


def _resident(shape):
    nd = len(shape)
    return pl.BlockSpec(shape, lambda *_: (0,) * nd, pipeline_mode=pl.Buffered(1))


def _layer(stacked, layer):
    return pl.BlockSpec((None,) + stacked.shape[1:], lambda *_: (layer, 0, 0), pipeline_mode=pl.Buffered(1))


def _div(x, k):
    assert k & (k - 1) == 0
    return jnp.right_shift(x, k.bit_length() - 1)


def _mod(x, k):
    assert k & (k - 1) == 0
    return jnp.bitwise_and(x, k - 1)


def _sigmoid(x):
    return 1.0 / (1.0 + jnp.exp(-x))


def _silu(x):
    return x * _sigmoid(x)


def _rmsnorm(x, gain):
    return x * lax.rsqrt(jnp.mean(x * x, axis=-1, keepdims=True) + EPS) * gain


def _dot(a, b):
    return jnp.dot(a, b, preferred_element_type=F32)


def _dot_nt(a, b):
    return lax.dot_general(a, b, (((1,), (1,)), ((), ())), preferred_element_type=F32)


def _ada_kernel(cond_ref, w_ref, b_ref, out_ref):
    s = _silu(cond_ref[...]).astype(BF16)
    out_ref[...] = _dot(s, w_ref[...].astype(BF16)) + b_ref[...]


def _ada_call(cond8, w_ada, b_ada):
    depth, d, n = w_ada.shape
    tn = n // ADA_COL_TILES
    return pl.pallas_call(
        _ada_kernel,
        grid=(depth, n // tn),
        in_specs=[
            pl.BlockSpec((SUBLANES, d), lambda l, j: (0, 0)),
            pl.BlockSpec((None, d, tn), lambda l, j: (l, 0, j)),
            pl.BlockSpec((None, 1, tn), lambda l, j: (l, 0, j)),
        ],
        out_specs=pl.BlockSpec((None, SUBLANES, tn), lambda l, j: (l, 0, j)),
        out_shape=jax.ShapeDtypeStruct((depth, SUBLANES, n), F32),
        compiler_params=_params("parallel", "parallel"),
        name="adaln_mod",
    )(cond8, w_ada, b_ada.reshape(depth, 1, n))


def _split_specs(tm, width, tiles0):
    return [pl.BlockSpec((tm, width), lambda i: (jnp.minimum(i, tiles0 - 1), 0)),
            pl.BlockSpec((tm, width), lambda i: (jnp.maximum(i - tiles0, 0), 0))]


def _load_split(ref0, ref1, tiles0):
    return jnp.where(pl.program_id(0) < tiles0, ref0[...], ref1[...])


def _store_split(ref0, ref1, tiles0, value):
    @pl.when(pl.program_id(0) < tiles0)
    def _():
        ref0[...] = value

    @pl.when(pl.program_id(0) >= tiles0)
    def _():
        ref1[...] = value


def _modulated(x0_ref, x1_ref, mod_ref, gain_ref, tiles0, d):
    shift = mod_ref[:, 0:d]
    scale = mod_ref[:, d:2 * d]
    x = _load_split(x0_ref, x1_ref, tiles0)
    return (_rmsnorm(x, gain_ref[...]) * (1.0 + scale) + shift).astype(BF16)


def _qkv_proj_kernel(x0_ref, x1_ref, mod_ref, gain_ref, w_ref, cos_ref, sa_ref, sb_ref, out_ref, *, d, tiles0):
    p = _dot(_modulated(x0_ref, x1_ref, mod_ref, gain_ref, tiles0, d), w_ref[...].astype(BF16))
    cos = cos_ref[...]
    sa = sa_ref[...]
    sb = sb_ref[...]
    for h in range(p.shape[1] // LANES):
        x = p[:, h * LANES:(h + 1) * LANES]
        if h < ATT_HEADS + KV_HEADS:
            x = x * cos + pltpu.roll(x, LANES - ROPE_SPAN, 1) * sa + pltpu.roll(x, ROPE_SPAN, 1) * sb
        out_ref[:, h * LANES:(h + 1) * LANES] = x


def _hgrn_proj_kernel(x0_ref, x1_ref, mod_ref, gain_ref, lb_ref, w_ref, out_ref, low_ref, *, d, tiles0):
    h = _modulated(x0_ref, x1_ref, mod_ref, gain_ref, tiles0, d)
    low = None
    for part in (0, 1, 2, 4, 3):
        cols = slice(part * d, (part + 1) * d)
        p = _dot(h, w_ref[:, cols].astype(BF16))
        if part == 0:
            p = _silu(p)
        elif part in (1, 2):
            lb = lb_ref[:, (part - 1) * d:part * d]
            p = jnp.log2(lb + (1.0 - lb) * _sigmoid(p))
            blocks = jnp.sum(p.reshape(p.shape[0] // GLA_BLOCK, GLA_BLOCK, d), axis=1)
            part_low = jnp.min(blocks, axis=0, keepdims=True)
            low = part_low if low is None else jnp.minimum(low, part_low)
        out_ref[:, cols] = p
    low_ref[...] = low


def _proj_call(x, mod, gain, w, layer, cond_row, tm, lb=None, rope=None):
    x0, x1 = x
    d = x0.shape[1]
    t = x0.shape[0] + x1.shape[0]
    tiles0 = x0.shape[0] // tm
    if lb is not None:
        body, name = _hgrn_proj_kernel, "hgrn_in_proj"
        extra_specs, extra_args = [_resident((1, 2 * d)), _layer(w, layer)], [lb, w]
    else:
        tables, n = rope
        per_seq = n // tm
        body, name = _qkv_proj_kernel, "attn_qkv_proj"
        table = pl.BlockSpec((tm, LANES), lambda i: (jnp.where(i < tiles0, per_seq, (i - tiles0) % per_seq), 0))
        extra_specs, extra_args = [_layer(w, layer), table, table, table], [w, *tables]
    out_specs = [pl.BlockSpec((tm, w.shape[2]), lambda i: (i, 0))]
    out_shape = [jax.ShapeDtypeStruct((t, w.shape[2]), F32)]
    if lb is not None:
        out_specs.append(pl.BlockSpec((None, 1, d), lambda i: (i, 0, 0)))
        out_shape.append(jax.ShapeDtypeStruct((t // tm, 1, d), F32))
    outs = pl.pallas_call(
        functools.partial(body, d=d, tiles0=tiles0),
        grid=(t // tm,),
        in_specs=_split_specs(tm, d, tiles0) + [
            pl.BlockSpec((None, 1, mod.shape[-1]), lambda i: (cond_row(i * tm), 0, 0)),
            _resident((1, d)),
        ] + extra_specs,
        out_specs=out_specs,
        out_shape=out_shape,
        compiler_params=_params("parallel"),
        name=name,
    )(x0, x1, mod, gain, *extra_args)
    return outs if lb is not None else outs[0]


def _ffn_kernel(x0_ref, x1_ref, a0_ref, a1_ref, mod_ref, gain_ref, wo_ref, wgu_ref, wd_ref, nf_ref, *rest,
                d, d_ff, ff_chunk, final, tiles0, gated):
    out_refs = rest[1:] if gated else rest
    g1 = mod_ref[:, 2 * d:3 * d]
    sh2 = mod_ref[:, 3 * d:4 * d]
    sc2 = mod_ref[:, 4 * d:5 * d]
    g2 = mod_ref[:, 5 * d:6 * d]
    a = _load_split(a0_ref, a1_ref, tiles0)
    if gated:
        a = a * _silu(rest[0][...])
    x1 = _load_split(x0_ref, x1_ref, tiles0) + g1 * _dot(a.astype(BF16), wo_ref[...])
    h = (_rmsnorm(x1, gain_ref[...]) * (1.0 + sc2) + sh2).astype(BF16)
    y = None
    for j in range(d_ff // ff_chunk):
        lo = j * ff_chunk
        g = _dot(h, wgu_ref[:, lo:lo + ff_chunk])
        u = _dot(h, wgu_ref[:, d_ff + lo:d_ff + lo + ff_chunk])
        part = _dot((_silu(g) * u).astype(BF16), wd_ref[lo:lo + ff_chunk, :])
        y = part if y is None else y + part
    x2 = x1 + g2 * y
    if final:
        x2 = _rmsnorm(x2, nf_ref[...])
    _store_split(out_refs[0], out_refs[1], tiles0, x2)


def _ffn_call(x, a, mod, gain, w_out, mixer_layer, w_gu, w_d, layer, norm_final, cond_row, tm, final, gate=None):
    (x0, x1), (a0, a1) = x, a
    d = x0.shape[1]
    d_ff = w_d.shape[1]
    ff_chunk = d_ff
    tiles0 = x0.shape[0] // tm
    rows = _split_specs(tm, d, tiles0)
    gate_specs, gate_args = [], []
    if gate is not None:
        gate_array, gate_block = gate
        gate_specs, gate_args = [pl.BlockSpec((tm, d), lambda i: (i, gate_block))], [gate_array]
    return pl.pallas_call(
        functools.partial(_ffn_kernel, d=d, d_ff=d_ff, ff_chunk=ff_chunk, final=final, tiles0=tiles0,
                          gated=gate is not None),
        grid=((x0.shape[0] + x1.shape[0]) // tm,),
        in_specs=rows + rows + [
            pl.BlockSpec((None, 1, mod.shape[-1]), lambda i: (cond_row(i * tm), 0, 0)),
            _resident((1, d)),
            _layer(w_out, mixer_layer),
            _layer(w_gu, layer),
            _layer(w_d, layer),
            _resident((1, d)),
        ] + gate_specs,
        out_specs=rows,
        out_shape=[jax.ShapeDtypeStruct(x0.shape, F32), jax.ShapeDtypeStruct(x1.shape, F32)],
        compiler_params=_params("arbitrary"),
        name="outproj_ffn",
    )(x0, x1, a0, a1, mod, gain, w_out, w_gu, w_d, norm_final, *gate_args)


def _row_blocks(b, rows, size):
    parts = [jnp.broadcast_to(b[r:r + 1, :], (size, b.shape[1])) for r in rows]
    return parts[0] if len(parts) == 1 else jnp.concatenate(parts, axis=0)


def _scores_blockwise(q, k, b, reverse):
    c, dk = q.shape
    blk = GLA_BLOCK
    nblk = c // blk
    zero_row = jnp.zeros((1, dk), F32)
    rows = []
    for i in range(nblk):
        lo, hi = i * blk, (i + 1) * blk
        if reverse:
            r = b[hi:hi + 1, :] if i < nblk - 1 else zero_row
            key_lo, key_hi = lo, c
        else:
            r = b[lo - 1:lo, :] if i > 0 else zero_row
            key_lo, key_hi = 0, hi
        qi = (q[lo:hi] * jnp.exp2(b[lo:hi] - r)).astype(BF16)
        ki = (k[key_lo:key_hi] * jnp.exp2(r - b[key_lo:key_hi])).astype(BF16)
        if key_hi - key_lo < c:
            pad = jnp.zeros((c - (key_hi - key_lo), dk), BF16)
            ki = jnp.concatenate([pad, ki] if reverse else [ki, pad], axis=0)
        rows.append(_dot_nt(qi, ki))
    return jnp.concatenate(rows, axis=0)


def _causal(c, reverse):
    t_idx = lax.broadcasted_iota(jnp.int32, (c, c), 0)
    s_idx = lax.broadcasted_iota(jnp.int32, (c, c), 1)
    return (s_idx >= t_idx) if reverse else (s_idx <= t_idx)


def _gla_decay_sums(lg, ones):
    hi = lg.astype(BF16)
    mid = (lg - hi.astype(F32)).astype(BF16)
    return _dot(ones, jnp.concatenate([hi, mid], axis=1))


def _gla_mix(q, lg, b2, v, st, tri, reverse, bounded):
    c = q.shape[0]
    dk = q.shape[1]
    k = 1.0 - jnp.exp2(lg)
    b = b2[:, 0:dk] + b2[:, dk:2 * dk]
    b_end = b[0:1, :] if reverse else b[c - 1:c, :]
    o = _dot_nt((q * jnp.exp2(b)).astype(BF16), st.astype(BF16))
    k_dec = (k * jnp.exp2(b_end - b)).astype(BF16)
    st_new = st * jnp.exp2(b_end) + _dot(v.T.astype(BF16), k_dec)
    if bounded:
        scores = jnp.where(tri, _scores_blockwise(q, k, b, reverse), 0.0)
    else:
        scores = _scores_robust(q, k, b, tri, reverse)
    return o, st_new, scores.astype(BF16)


def _scores_robust(q, k, b, tri, reverse):
    c, dk = q.shape
    t_idx = lax.broadcasted_iota(jnp.int32, (c, c), 0)
    s_idx = lax.broadcasted_iota(jnp.int32, (c, c), 1)
    pos = lax.broadcasted_iota(jnp.int32, (c, dk), 0)
    scores = jnp.zeros((c, c), F32)
    m = c // 2
    while m >= GLA_LEAF:
        nblk = c // (2 * m)
        bound = [kb * 2 * m + (m if reverse else m - 1) for kb in range(nblk)]
        e = jnp.exp2(-jnp.abs(b - _row_blocks(b, bound, 2 * m)))
        late = _mod(pos, 2 * m) >= m
        q_side = jnp.logical_not(late) if reverse else late
        ql = jnp.where(q_side, q * e, 0.0).astype(BF16)
        kl = jnp.where(q_side, 0.0, k * e).astype(BF16)
        s_l = _dot_nt(ql, kl)
        if nblk > 1:
            s_l = jnp.where(_div(t_idx, 2 * m) == _div(s_idx, 2 * m), s_l, 0.0)
        scores = scores + s_l
        m //= 2
    nleaf = c // GLA_LEAF
    k3 = k.reshape(nleaf, GLA_LEAF, dk)
    b3 = b.reshape(nleaf, GLA_LEAF, dk)
    sel_d = _mod(lax.broadcasted_iota(jnp.int32, (dk, c), 1), GLA_LEAF)
    r = jnp.zeros((c, c), F32)
    for sl in range(GLA_LEAF):
        ks = jnp.broadcast_to(k3[:, sl:sl + 1, :], k3.shape).reshape(c, dk)
        bs = jnp.broadcast_to(b3[:, sl:sl + 1, :], b3.shape).reshape(c, dk)
        term = (q * ks * jnp.exp2(jnp.minimum(b - bs, 0.0))).astype(BF16)
        r = r + _dot(term, jnp.where(sel_d == sl, 1.0, 0.0).astype(BF16))
    same_leaf = _div(t_idx, GLA_LEAF) == _div(s_idx, GLA_LEAF)
    return scores + jnp.where(jnp.logical_and(same_leaf, tri), r, 0.0)


def _gla_kernel(*refs, n, hb, has_s0, nprev, want_state):
    q_ref, lgf_ref, lgb_ref, v_ref, gn_ref, low_ref = refs[:6]
    pos = 6
    s0_ref = prev_ref = sfin_ref = None
    if has_s0:
        s0_ref = refs[pos]
        pos += 1
    if nprev:
        prev_ref = refs[pos]
        pos += 1
    o_ref = refs[pos]
    pos += 1
    if want_state:
        sfin_ref = refs[pos]
        pos += 1
    half_ref, st_ref = refs[pos:pos + 2]
    c = GLA_CHUNK
    hd = LANES
    nchunks = n // c

    for direction in (0, 1):
        for h in range(hb):
            if has_s0:
                st_ref[direction, h] = s0_ref[direction, h].T
            else:
                st_ref[direction, h] = jnp.zeros((hd, hd), F32)

    bounded = jnp.min(low_ref[...]) >= -GLA_RANGE_BOUND

    def scan(is_bounded):
        def body(i, carry, closing):
            tri = (_causal(c, False), _causal(c, True))
            ones = tuple(jnp.where(t, 1.0, 0.0).astype(BF16) for t in tri)
            chains = []
            for h in range(hb):
                cols = slice(h * hd, (h + 1) * hd)
                for direction in (0, 1):
                    ci = i if direction == 0 else nchunks - 1 - i
                    rows = pl.ds(pl.multiple_of(ci * c, c), c)
                    lg = (lgf_ref if direction == 0 else lgb_ref)[rows, cols]
                    chains.append((h, direction, rows, cols, lg, _gla_decay_sums(lg, ones[direction])))
            mixed = []
            for h, direction, rows, cols, lg, b2 in chains:
                mixed.append(_gla_mix(q_ref[rows, cols], lg, b2, v_ref[rows, cols], st_ref[direction, h],
                                      tri[direction], direction == 1, is_bounded))
            for (h, direction, rows, cols, _, _), (o, st_new, scores) in zip(chains, mixed):
                st_ref[direction, h] = st_new
                o = o + _dot(scores, v_ref[rows, cols].astype(BF16))
                if closing:
                    o = o + half_ref[rows, cols]
                    o = o * lax.rsqrt(jnp.mean(o * o, axis=-1, keepdims=True) + EPS) * gn_ref[:, cols]
                    o_ref[rows, cols] = o
                else:
                    half_ref[rows, cols] = o
            return carry

        lax.fori_loop(0, nchunks // 2, functools.partial(body, closing=False), 0)
        lax.fori_loop(nchunks // 2, nchunks, functools.partial(body, closing=True), 0)
        if want_state:
            for direction in (0, 1):
                for h in range(hb):
                    sfin_ref[nprev, direction, h] = st_ref[direction, h].T

    if want_state and nprev:
        sfin_ref[0:nprev] = prev_ref[...]
    pl.when(bounded)(functools.partial(scan, True))
    pl.when(jnp.logical_not(bounded))(functools.partial(scan, False))


def _gla_call(proj, low, gnorm, s0, prev, row0, nseq, n, hb, want_state, name):
    hd = LANES
    heads = A_HEADS
    hblocks = heads // hb
    assert n % (2 * GLA_CHUNK) == 0
    blk0 = row0 // n

    assert (2 * 5 + 1) * n * hb * hd * 4 <= V7X_VMEM_LIMIT_BYTES - GLA_VMEM_HEADROOM_BYTES

    def col(kind):
        return pl.BlockSpec((n, hb * hd), lambda b, h: (blk0 + b, kind * hblocks + h))

    in_specs = [col(0), col(1), col(2), col(3),
                pl.BlockSpec((1, hb * hd), lambda b, h: (0, h)),
                pl.BlockSpec((n // HGRN_PROJ_TILE, 1, hb * hd), lambda b, h: (blk0 + b, 0, h))]
    args = [proj, proj, proj, proj, gnorm, low]
    state_spec = pl.BlockSpec((None, 2, hb, hd, hd), lambda b, h: (b, 0, h, 0, 0))
    if s0 is not None:
        in_specs.append(state_spec)
        args.append(s0)
    nprev = 0 if prev is None else prev.shape[1]

    def layered(layers):
        return pl.BlockSpec((None, layers, 2, hb, hd, hd), lambda b, h: (b, 0, 0, h, 0, 0))

    if nprev:
        in_specs.append(layered(nprev))
        args.append(prev)
    out_shape = [jax.ShapeDtypeStruct((nseq * n, heads * hd), F32)]
    out_specs = [pl.BlockSpec((n, hb * hd), lambda b, h: (b, h))]
    if want_state:
        out_shape.append(jax.ShapeDtypeStruct((nseq, nprev + 1, 2, heads, hd, hd), F32))
        out_specs.append(layered(nprev + 1))
    return pl.pallas_call(
        functools.partial(_gla_kernel, n=n, hb=hb, has_s0=s0 is not None, nprev=nprev, want_state=want_state),
        grid=(nseq, hblocks),
        in_specs=in_specs,
        out_specs=out_specs,
        out_shape=out_shape,
        scratch_shapes=[pltpu.VMEM((n, hb * hd), F32), pltpu.VMEM((2, hb, hd, hd), F32)],
        compiler_params=_params("parallel", "parallel"),
        name=name,
    )(*args)


def _stacked_queries(q_ref, kh, scale):
    heads = [q_ref[:, (kh * GROUP + g) * LANES:(kh * GROUP + g + 1) * LANES] for g in range(GROUP)]
    return (jnp.concatenate(heads, axis=0) * scale).astype(BF16)


def _sink_column(sink_ref, kh, rows):
    gi = _div(lax.broadcasted_iota(jnp.int32, (GROUP * rows, 1), 0), rows)
    col = jnp.zeros((GROUP * rows, 1), F32)
    for g in range(GROUP):
        col = jnp.where(gi == g, sink_ref[kh, g] * LOG2_E, col)
    return col


def _softmax_pv(logits, sinks, values):
    mx = [jnp.maximum(jnp.max(lg, axis=-1, keepdims=True), s) for lg, s in zip(logits, sinks)]
    p = [jnp.exp2(lg - m) for lg, m in zip(logits, mx)]
    den = [jnp.sum(pi, axis=-1, keepdims=True) + jnp.exp2(s - m) for pi, s, m in zip(p, sinks, mx)]
    return [_dot(pi.astype(BF16), v) * (1.0 / d) for pi, v, d in zip(p, values, den)]


def _store_heads(o_ref, kh, o):
    rows = o_ref.shape[0]
    for g in range(GROUP):
        o_ref[:, (kh * GROUP + g) * LANES:(kh * GROUP + g + 1) * LANES] = o[g * rows:(g + 1) * rows, :]


def _attn_ctx_kernel(sink_ref, q_ref, k_ref, v_ref, *rest, scale, nprev):
    if nprev:
        pk_ref, pv_ref, o_ref, nk_ref, nv_ref = rest
        nk_ref[0:nprev] = pk_ref[...]
        nv_ref[0:nprev] = pv_ref[...]
    else:
        o_ref, nk_ref, nv_ref = rest
    for kh in range(KV_HEADS):
        nk_ref[nprev, :, kh, :] = k_ref[:, kh * LANES:(kh + 1) * LANES]
        nv_ref[nprev, :, kh, :] = v_ref[:, kh * LANES:(kh + 1) * LANES]
    rows = q_ref.shape[0]
    heads = range(KV_HEADS)
    q = [_stacked_queries(q_ref, kh, scale) for kh in heads]
    logits = [_dot_nt(q[kh], k_ref[:, kh * LANES:(kh + 1) * LANES].astype(BF16)) for kh in heads]
    sinks = [_sink_column(sink_ref, kh, rows) for kh in heads]
    values = [v_ref[:, kh * LANES:(kh + 1) * LANES].astype(BF16) for kh in heads]
    for kh, o in zip(heads, _softmax_pv(logits, sinks, values)):
        _store_heads(o_ref, kh, o)


def _attn_ctx_call(qkv, sink, prev_k, prev_v, nseq, n):
    hd = LANES
    kvw = KV_HEADS * hd
    nprev = 0 if prev_k is None else prev_k.shape[1]

    def layered(layers):
        return pl.BlockSpec((None, layers, n, KV_HEADS, hd), lambda b: (b, 0, 0, 0, 0))

    kv_shape = jax.ShapeDtypeStruct((nseq, nprev + 1, n, KV_HEADS, hd), F32)
    prev_specs = [layered(nprev), layered(nprev)] if nprev else []
    prev_args = [prev_k, prev_v] if nprev else []
    return pl.pallas_call(
        functools.partial(_attn_ctx_kernel, scale=hd ** -0.5 * LOG2_E, nprev=nprev),
        grid=(nseq,),
        in_specs=[
            pl.BlockSpec(memory_space=pltpu.SMEM),
            pl.BlockSpec((n, ATT_HEADS * hd), lambda b: (b, 0)),
            pl.BlockSpec((n, kvw), lambda b: (b, ATT_HEADS // KV_HEADS)),
            pl.BlockSpec((n, kvw), lambda b: (b, ATT_HEADS // KV_HEADS + 1)),
        ] + prev_specs,
        out_specs=[pl.BlockSpec((n, ATT_HEADS * hd), lambda b: (b, 0)), layered(nprev + 1), layered(nprev + 1)],
        out_shape=[jax.ShapeDtypeStruct((nseq * n, ATT_HEADS * hd), F32), kv_shape, kv_shape],
        compiler_params=_params("parallel"),
        name="attn_context",
    )(sink, qkv, qkv, qkv, *prev_args)


def _attn_lat_kernel(sink_ref, bias_ref, q_ref, k0_ref, k1_ref, k2_ref, v0_ref, v1_ref, v2_ref, kc_ref, vc_ref,
                     o_ref, *, scale, nblk):
    nb = pl.program_id(1)
    rows = q_ref.shape[0]
    past = kc_ref.shape[0]
    heads = range(KV_HEADS)
    bias_before = bias_ref[:, 0:BLOCK] + jnp.where(nb == 0, NEG_BIG, 0.0)
    bias_after = bias_ref[:, BLOCK:] + jnp.where(nb == nblk - 1, NEG_BIG, 0.0)

    def rows_of(kh, refs):
        return jnp.concatenate([r[:, kh * LANES:(kh + 1) * LANES] for r in refs], axis=0).astype(BF16)

    half = GROUP * rows // 2
    chains = [(kh, part) for kh in heads for part in range(2)]

    def part_of(x, part):
        return x[part * half:(part + 1) * half]

    q = [_stacked_queries(q_ref, kh, scale) for kh in heads]
    keys = [rows_of(kh, (kc_ref, k0_ref, k1_ref, k2_ref)) for kh in heads]
    raw = [_dot_nt(part_of(q[kh], part), keys[kh]) for kh, part in chains]
    logits = [jnp.concatenate([lg[:, 0:past], lg[:, past:past + BLOCK] + part_of(bias_before, part),
                               lg[:, past + BLOCK:past + 2 * BLOCK],
                               lg[:, past + 2 * BLOCK:] + part_of(bias_after, part)], axis=1)
              for lg, (kh, part) in zip(raw, chains)]
    sinks = [part_of(_sink_column(sink_ref, kh, rows), part) for kh, part in chains]
    values = [rows_of(kh, (vc_ref, v0_ref, v1_ref, v2_ref)) for kh in heads]
    outs = _softmax_pv(logits, sinks, [values[kh] for kh, _ in chains])
    for kh in heads:
        _store_heads(o_ref, kh, jnp.concatenate(outs[2 * kh:2 * kh + 2], axis=0))


def _attn_lat_call(qkv, cache_k, cache_v, sink, row0, nseq, n):
    hd = LANES
    nblk = n // BLOCK
    past = cache_k.shape[1]
    kcol = ATT_HEADS // KV_HEADS
    blk0 = row0 // BLOCK

    def near(off, colblock):
        def index(b, i):
            return (blk0 + b * nblk + jnp.clip(i + off, 0, nblk - 1), colblock)
        return pl.BlockSpec((BLOCK, KV_HEADS * hd), index)

    t_in = np.arange(GROUP * BLOCK)[:, None] % BLOCK
    j_in = np.arange(BLOCK)[None, :]
    bias = jnp.asarray(np.concatenate([np.where(j_in >= t_in, 0.0, NEG_BIG), np.where(j_in <= t_in, 0.0, NEG_BIG)],
                                      axis=1).astype(np.float32))
    cache_spec = pl.BlockSpec((None, past, KV_HEADS * hd), lambda b, i: (b, 0, 0))
    return pl.pallas_call(
        functools.partial(_attn_lat_kernel, scale=hd ** -0.5 * LOG2_E, nblk=nblk),
        grid=(nseq, nblk),
        in_specs=[
            pl.BlockSpec(memory_space=pltpu.SMEM),
            pl.BlockSpec(bias.shape, lambda b, i: (0, 0)),
            pl.BlockSpec((BLOCK, ATT_HEADS * hd), lambda b, i: (blk0 + b * nblk + i, 0)),
            near(-1, kcol), near(0, kcol), near(1, kcol),
            near(-1, kcol + 1), near(0, kcol + 1), near(1, kcol + 1),
            cache_spec, cache_spec,
        ],
        out_specs=pl.BlockSpec((BLOCK, ATT_HEADS * hd), lambda b, i: (b * nblk + i, 0)),
        out_shape=jax.ShapeDtypeStruct((nseq * n, ATT_HEADS * hd), F32),
        compiler_params=_params("parallel", "parallel"),
        name="attn_latent",
    )(sink, bias, qkv, qkv, qkv, qkv, qkv, qkv, qkv, cache_k, cache_v)


def _rope_tables(n, pad):
    quarter = ROPE_SPAN
    inv = ROPE_BASE ** (-np.arange(quarter, dtype=np.float64) / quarter)
    pos = np.arange(n)
    ang_r = (pos // GRID_W)[:, None] * inv[None, :]
    ang_c = (pos % GRID_W)[:, None] * inv[None, :]
    zero = np.zeros_like(ang_r)
    cos = np.concatenate([np.cos(ang_r), np.cos(ang_r), np.cos(ang_c), np.cos(ang_c)], axis=1)
    sin_up = np.concatenate([-np.sin(ang_r), zero, -np.sin(ang_c), zero], axis=1)
    sin_dn = np.concatenate([zero, np.sin(ang_r), zero, np.sin(ang_c)], axis=1)
    ones, zeros = np.ones((pad, LANES)), np.zeros((pad, LANES))
    return tuple(jnp.asarray(np.concatenate(t).astype(np.float32))
                 for t in ((cos, ones), (sin_up, zeros), (sin_dn, zeros)))


def _lower_bounds(lb_param):
    p = jax.nn.softmax(lb_param.astype(F32), axis=0)
    cs = jnp.cumsum(p, axis=0)
    return cs - cs[:1]


def kernel(x_prompt, x_sample, cache_k, cache_v, state_hgrn, c, c_ctx, w_ada, b_ada, norm1, norm2, norm_final,
           w_gate_up, w_down, w_in_a, lower_bounds, gnorm_a, w_out_a, w_qkv_b, w_out_b, sink_b):
    batch, seq, d = x_prompt.shape
    dec_batch, dec_seq, _ = x_sample.shape
    depth = w_ada.shape[0]
    t_ctx = batch * seq
    t_lat = dec_batch * dec_seq
    assert dec_batch + 1 <= SUBLANES
    assert all(t_ctx % tile == 0 and dec_seq % tile == 0 for tile in (TOKEN_TILE, QKV_PROJ_TILE, HGRN_PROJ_TILE))
    assert WINDOW == BLOCK

    def cond_row(tok0):
        return jnp.where(tok0 < t_ctx, 0, 1 + (tok0 - t_ctx) // dec_seq)

    x = (x_prompt.reshape(t_ctx, d), x_sample.reshape(t_lat, d))
    cond8 = jnp.concatenate([c_ctx[None, :], c, jnp.zeros((SUBLANES - 1 - dec_batch, d), F32)], axis=0)
    mod = _ada_call(cond8, w_ada, b_ada).reshape(depth, SUBLANES, 1, 6 * d)
    lb_all = _lower_bounds(lower_bounds)
    rope_tabs = _rope_tables(dec_seq, QKV_PROJ_TILE)
    past = cache_k.shape[2]
    nf = norm_final.reshape(1, d)

    w_out_a, w_out_b, w_gate_up, w_down = (w.astype(BF16) for w in (w_out_a, w_out_b, w_gate_up, w_down))
    new_k = new_v = new_s = None
    for l in range(depth):
        j = l // 2
        n1 = norm1[l].reshape(1, d)
        n2 = norm2[l].reshape(1, d)
        if l % 2 == 0:
            proj, low = _proj_call(x, mod[l], n1, w_in_a, j, cond_row, HGRN_PROJ_TILE,
                                   lb_all[j].reshape(1, 2 * d))
            gn = gnorm_a[j].reshape(1, d)
            a_ctx, new_s = _gla_call(proj, low, gn, None, new_s, 0, batch, seq, GLA_HEADS_CTX, True,
                                     "hgrn_scan_ctx")
            (a_lat,) = _gla_call(proj, low, gn, state_hgrn[:, j], None, t_ctx, dec_batch, dec_seq, GLA_HEADS_LAT,
                                 False, "hgrn_scan_lat")
            w_mix, gate = w_out_a, (proj, 4)
        else:
            qkv = _proj_call(x, mod[l], n1, w_qkv_b, j, cond_row, QKV_PROJ_TILE, rope=(rope_tabs, dec_seq))
            sink = sink_b[j].reshape(KV_HEADS, GROUP)
            a_ctx, new_k, new_v = _attn_ctx_call(qkv, sink, new_k, new_v, batch, seq)
            a_lat = _attn_lat_call(qkv, cache_k[:, j].reshape(dec_batch, past, KV_HEADS * LANES),
                                   cache_v[:, j].reshape(dec_batch, past, KV_HEADS * LANES), sink,
                                   t_ctx, dec_batch, dec_seq)
            w_mix, gate = w_out_b, None
        x = _ffn_call(x, (a_ctx, a_lat), mod[l], n2, w_mix, j, w_gate_up, w_down, l, nf, cond_row, TOKEN_TILE,
                      l == depth - 1, gate)

    y_prompt = x[0].reshape(batch, seq, d)
    y_sample = x[1].reshape(dec_batch, dec_seq, d)
    return (y_prompt, y_sample, new_k, new_v, new_s)
```
